```python
import functools
import jax, jax.numpy as jnp
from jax import lax
import numpy as np

D_MODEL = 4096
BATCH = 1
SEQ = 8192
DEPTH = 1
DEC_BATCH = 128
DEC_SEQ = 8
PAST_LEN = 2048
PAGE_SIZE = 128

N_META = 16
HGRN_HEADS = D_MODEL // 256
HGRN_DK = 128
HGRN_DV = 128
HGRN_CHUNK = 128
ATT_HEADS = D_MODEL // 256
HEAD_DIM = 128
Q_RANK = D_MODEL // 4
IDX_HEADS = D_MODEL // 128
IDX_DIM = 128
TOPK_MAX = 256
ATT_BLOCK = 128
D_FF = -(-8 * D_MODEL // (3 * 256)) * 256
EPS = 1e-6

D_R = HGRN_HEADS * HGRN_DK
D_RV = HGRN_HEADS * HGRN_DV
D_ATT = ATT_HEADS * HEAD_DIM
D_MIX = D_RV + D_ATT
IN_SPLITS = (D_R, D_R, D_RV, D_RV, Q_RANK, D_ATT, D_ATT, IDX_DIM, IDX_HEADS)
D_IN = D_R * 2 + D_RV * 2 + Q_RANK + D_ATT * 2 + IDX_DIM + IDX_HEADS

kernel_name = "hymba_hgrn2_dsa_decoder_step"


def rmsnorm(x, g):
    xf = x.astype(jnp.float32)
    y = xf * lax.rsqrt(jnp.mean(xf * xf, axis=-1, keepdims=True) + EPS)
    return (y * g.astype(jnp.float32)).astype(x.dtype)


def layernorm(x, g, b):
    xf = x.astype(jnp.float32)
    mu = jnp.mean(xf, axis=-1, keepdims=True)
    var = jnp.mean(jnp.square(xf - mu), axis=-1, keepdims=True)
    return ((xf - mu) * lax.rsqrt(var + EPS) * g.astype(jnp.float32) + b.astype(jnp.float32)).astype(x.dtype)


def split_projection(hn, w_in):
    z = hn @ w_in
    bounds = np.cumsum(IN_SPLITS)[:-1].tolist()
    return jnp.split(z, bounds, axis=-1)


def hgrn_inputs(q_r, f_r, i_r, lb):
    B, L = q_r.shape[:2]
    ff = f_r.astype(jnp.float32)
    lbf = lb.astype(jnp.float32)
    log_f = jnp.logaddexp(jnp.log(lbf), jnp.log1p(-lbf) + jax.nn.log_sigmoid(ff))
    k = (1.0 - lbf) * jax.nn.sigmoid(-ff)
    q = jax.nn.silu(q_r.astype(jnp.float32))
    hk = lambda t: t.reshape(B, L, HGRN_HEADS, HGRN_DK)
    v = i_r.astype(jnp.float32).reshape(B, L, HGRN_HEADS, HGRN_DV)
    return hk(q), hk(k), v, hk(log_f)


def hgrn2_chunk(S0, q, k, v, log_f):
    C = q.shape[1]
    G = jnp.cumsum(log_f, axis=1)
    causal = jnp.tril(jnp.ones((C, C), dtype=bool))[None, :, :, None, None]
    diff = G[:, :, None] - G[:, None, :]
    decay = jnp.exp(jnp.where(causal, diff, -jnp.inf))
    A = jnp.einsum('bthk,btshk,bshk->bhts', q, decay, k)
    o = jnp.einsum('bhts,bshv->bthv', A, v) + jnp.einsum('bthk,bhkv->bthv', q * jnp.exp(G), S0)
    G_last = G[:, -1]
    S = jnp.exp(G_last)[..., None] * S0 + jnp.einsum('bshk,bshv->bhkv', k * jnp.exp(G_last[:, None] - G), v)
    return S, o


def att_inputs(cq, k_r, v_r, ik_r, iw_r, qn_g, w_uq, w_uq_idx, ikn_g, ikn_b):
    B, L = cq.shape[:2]
    cqn = rmsnorm(cq, qn_g)
    q = (cqn @ w_uq).reshape(B, L, ATT_HEADS, HEAD_DIM)
    q_idx = (cqn @ w_uq_idx).reshape(B, L, IDX_HEADS, IDX_DIM)
    k = k_r.reshape(B, L, ATT_HEADS, HEAD_DIM)
    v = v_r.reshape(B, L, ATT_HEADS, HEAD_DIM)
    k_idx = layernorm(ik_r, ikn_g, ikn_b)
    w_idx = iw_r * (IDX_HEADS ** -0.5 * IDX_DIM ** -0.5)
    return q, k, v, q_idx, k_idx, w_idx


def index_select(q_idx, w_idx, k_idx, q_pos, n_sel):
    s = jnp.einsum('bqhd,bsd->bqhs', q_idx.astype(jnp.float32), k_idx.astype(jnp.float32))
    score = jnp.einsum('bqhs,bqh->bqs', jax.nn.relu(s), w_idx.astype(jnp.float32))
    k_pos = jnp.arange(k_idx.shape[1])
    score = jnp.where(k_pos[None, None, :] <= q_pos[None, :, None], score, -jnp.inf)
    _, idx = lax.top_k(score, n_sel)
    valid = idx <= q_pos[None, :, None]
    return idx, valid


def sparse_attend(q, k_sel, v_sel, valid):
    s = jnp.einsum('bqhd,bqkhd->bqhk', q.astype(jnp.float32), k_sel.astype(jnp.float32)) * (HEAD_DIM ** -0.5)
    s = jnp.where(valid[:, :, None, :], s, -jnp.inf)
    p = jax.nn.softmax(s, axis=-1)
    return jnp.einsum('bqhk,bqkhd->bqhd', p, v_sel.astype(jnp.float32)).astype(q.dtype)


def prompt_hgrn(q, k, v, log_f):
    B, T = q.shape[:2]
    S0 = jnp.zeros((B, HGRN_HEADS, HGRN_DK, HGRN_DV), jnp.float32)
    S, o_meta = hgrn2_chunk(S0, q[:, :N_META], k[:, :N_META], v[:, :N_META], log_f[:, :N_META])
    n_c = (T - N_META) // HGRN_CHUNK
    to_chunks = lambda t: jnp.moveaxis(t[:, N_META:].reshape(B, n_c, HGRN_CHUNK, *t.shape[2:]), 1, 0)
    S, o_c = lax.scan(lambda s, xs: hgrn2_chunk(s, *xs), S, (to_chunks(q), to_chunks(k), to_chunks(v), to_chunks(log_f)))
    o_c = jnp.moveaxis(o_c, 0, 1).reshape(B, T - N_META, HGRN_HEADS, HGRN_DV)
    return jnp.concatenate([o_meta, o_c], axis=1), S


def sample_hgrn(S_prev, q, k, v, log_f):
    S, o = hgrn2_chunk(S_prev.astype(jnp.float32), q, k, v, log_f)
    return o, S


def prompt_attention(qa, ka, va, qi, ki, wi):
    B, T = qa.shape[:2]
    n_sel = min(TOPK_MAX, (T - N_META) // 4)
    gather = jax.vmap(lambda a, i: a[i])

    def attend_block(args):
        qb, qib, wb, pos = args
        idx, valid = index_select(qib, wb, ki, pos, n_sel)
        return sparse_attend(qb, gather(ka, idx), gather(va, idx), valid)

    pos_all = jnp.arange(T)
    o_meta = attend_block((qa[:, :N_META], qi[:, :N_META], wi[:, :N_META], pos_all[:N_META]))
    n_b = (T - N_META) // ATT_BLOCK
    to_blocks = lambda t: jnp.moveaxis(t[:, N_META:].reshape(B, n_b, ATT_BLOCK, *t.shape[2:]), 1, 0)
    o_blk = lax.map(attend_block, (to_blocks(qa), to_blocks(qi), to_blocks(wi), pos_all[N_META:].reshape(n_b, ATT_BLOCK)))
    o_blk = jnp.moveaxis(o_blk, 0, 1).reshape(B, T - N_META, ATT_HEADS, HEAD_DIM)
    return jnp.concatenate([o_meta, o_blk], axis=1)


def sample_attention(ck, cv, cik, page_table, qa, ka, va, qi, ki, wi):
    DB, Ls = qa.shape[:2]
    page = ck.shape[1]
    past = page_table.shape[1] * page
    ki_past = cik[page_table].reshape(DB, past, IDX_DIM).astype(ki.dtype)
    ki_all = jnp.concatenate([ki_past, ki], axis=1)
    q_pos = past + jnp.arange(Ls)
    n_sel = min(TOPK_MAX, (past + Ls) // 4)
    idx, valid = index_select(qi, wi, ki_all, q_pos, n_sel)
    gather = jax.vmap(lambda a, i: a[i])

    def attend_token(args):
        q_j, idx_j, valid_j = args
        in_past = (idx_j < past)[..., None, None]
        pidx = jnp.minimum(idx_j, past - 1)
        phys = jnp.take_along_axis(page_table, pidx // page, axis=1)
        off = pidx % page
        nidx = jnp.clip(idx_j - past, 0, Ls - 1)
        sel = lambda c, new: jnp.where(in_past, c[phys, off].astype(new.dtype), gather(new, nidx))
        return sparse_attend(q_j[:, None], sel(ck, ka)[:, None], sel(cv, va)[:, None], valid_j[:, None])[:, 0]

    o = lax.map(attend_token, (jnp.moveaxis(qa, 1, 0), jnp.moveaxis(idx, 1, 0), jnp.moveaxis(valid, 1, 0)))
    return jnp.moveaxis(o, 0, 1)


def decoder_layer(h, lb, weights, run_hgrn, run_att):
    (nm_g, w_in, qn_g, w_uq, w_uq_idx, ikn_g, ikn_b, hgrn_g, w_out, nf_g, w_gate, w_up, w_down) = weights
    B, L = h.shape[:2]
    hn = rmsnorm(h, nm_g)
    q_r, f_r, i_r, og_r, cq, k_r, v_r, ik_r, iw_r = split_projection(hn, w_in)
    o_r, S = run_hgrn(*hgrn_inputs(q_r, f_r, i_r, lb))
    qa, ka, va, qi, ki, wi = att_inputs(cq, k_r, v_r, ik_r, iw_r, qn_g, w_uq, w_uq_idx, ikn_g, ikn_b)
    o_a = run_att(qa, ka, va, qi, ki, wi)
    o_r = rmsnorm(o_r.astype(h.dtype), hgrn_g).reshape(B, L, D_RV) * jax.nn.silu(og_r)
    o = jnp.concatenate([o_r, o_a.reshape(B, L, D_ATT)], axis=-1)
    h = h + o @ w_out
    hf = rmsnorm(h, nf_g)
    h = h + (jax.nn.silu(hf @ w_gate) * (hf @ w_up)) @ w_down
    return h, ka, va, ki, S


def setup_inputs(seed: int = 0) -> dict:
    key = jax.random.key(seed)
    ks = jax.random.split(key, 24)
    n_pages = PAST_LEN // PAGE_SIZE
    n_used = DEC_BATCH * n_pages
    n_pool = n_used + max(1, n_used // 4)
    nrm = lambda k, shape, s=1.0: jax.random.normal(k, shape, jnp.float32) * s
    gain = lambda k, shape: 1.0 + 0.02 * jax.random.normal(k, shape, jnp.float32)
    page_table = jax.random.permutation(ks[6], n_pool)[:n_used].reshape(DEC_BATCH, n_pages).astype(jnp.int32)
    return {
        "x_prompt": nrm(ks[0], (BATCH, SEQ, D_MODEL)),
        "x_sample": nrm(ks[1], (DEC_BATCH, DEC_SEQ, D_MODEL)),
        "cache_k": nrm(ks[2], (DEPTH, n_pool, PAGE_SIZE, ATT_HEADS, HEAD_DIM)),
        "cache_v": nrm(ks[3], (DEPTH, n_pool, PAGE_SIZE, ATT_HEADS, HEAD_DIM)),
        "cache_idx_k": nrm(ks[4], (DEPTH, n_pool, PAGE_SIZE, IDX_DIM)),
        "state_hgrn": nrm(ks[5], (DEPTH, DEC_BATCH, HGRN_HEADS, HGRN_DK, HGRN_DV), 0.5),
        "page_table": page_table,
        "meta_tokens": nrm(ks[7], (N_META, D_MODEL)),
        "lb_logits": nrm(ks[8], (DEPTH + 1, D_R), 0.1),
        "norm_mix_g": gain(ks[9], (DEPTH, D_MODEL)),
        "w_in": nrm(ks[10], (DEPTH, D_MODEL, D_IN), D_MODEL ** -0.5),
        "q_norm_g": gain(ks[11], (DEPTH, Q_RANK)),
        "w_uq": nrm(ks[12], (DEPTH, Q_RANK, D_ATT), Q_RANK ** -0.5),
        "w_uq_idx": nrm(ks[13], (DEPTH, Q_RANK, IDX_HEADS * IDX_DIM), Q_RANK ** -0.5),
        "idx_k_norm_g": gain(ks[14], (DEPTH, IDX_DIM)),
        "idx_k_norm_b": nrm(ks[15], (DEPTH, IDX_DIM), 0.02),
        "hgrn_norm_g": gain(ks[16], (DEPTH, HGRN_DV)),
        "w_out": nrm(ks[17], (DEPTH, D_MIX, D_MODEL), D_MIX ** -0.5),
        "norm_ffn_g": gain(ks[18], (DEPTH, D_MODEL)),
        "w_gate": nrm(ks[19], (DEPTH, D_MODEL, D_FF), D_MODEL ** -0.5),
        "w_up": nrm(ks[20], (DEPTH, D_MODEL, D_FF), D_MODEL ** -0.5),
        "w_down": nrm(ks[21], (DEPTH, D_FF, D_MODEL), D_FF ** -0.5),
        "final_norm_g": gain(ks[22], (D_MODEL,)),
    }


def reference(x_prompt, x_sample, cache_k, cache_v, cache_idx_k, state_hgrn, page_table, meta_tokens,
              lb_logits, norm_mix_g, w_in, q_norm_g, w_uq, w_uq_idx, idx_k_norm_g, idx_k_norm_b,
              hgrn_norm_g, w_out, norm_ffn_g, w_gate, w_up, w_down, final_norm_g):
    B = x_prompt.shape[0]
    meta = jnp.broadcast_to(meta_tokens.astype(x_prompt.dtype)[None], (B, N_META, meta_tokens.shape[-1]))
    hp = jnp.concatenate([meta, x_prompt], axis=1)
    hs = x_sample
    lb_all = jnp.cumsum(jax.nn.softmax(lb_logits.astype(jnp.float32), axis=0), axis=0)
    kp, vp, ikp, sp, kss, vss, iks, sss = [], [], [], [], [], [], [], []
    for l in range(DEPTH):
        weights = (norm_mix_g[l], w_in[l], q_norm_g[l], w_uq[l], w_uq_idx[l], idx_k_norm_g[l], idx_k_norm_b[l],
                   hgrn_norm_g[l], w_out[l], norm_ffn_g[l], w_gate[l], w_up[l], w_down[l])
        hp, k_l, v_l, ik_l, s_l = decoder_layer(hp, lb_all[l], weights, prompt_hgrn, prompt_attention)
        kp.append(k_l); vp.append(v_l); ikp.append(ik_l); sp.append(s_l.astype(x_prompt.dtype))
        run_att = functools.partial(sample_attention, cache_k[l], cache_v[l], cache_idx_k[l], page_table)
        run_hgrn = functools.partial(sample_hgrn, state_hgrn[l])
        hs, k_l, v_l, ik_l, s_l = decoder_layer(hs, lb_all[l], weights, run_hgrn, run_att)
        kss.append(k_l); vss.append(v_l); iks.append(ik_l); sss.append(s_l.astype(state_hgrn.dtype))
    y_prompt = rmsnorm(hp, final_norm_g)[:, N_META:]
    y_sample = rmsnorm(hs, final_norm_g)
    return (y_prompt, y_sample, jnp.stack(kp), jnp.stack(vp), jnp.stack(ikp), jnp.stack(sp),
            jnp.stack(kss), jnp.stack(vss), jnp.stack(iks), jnp.stack(sss))
```

```python
import functools
from typing import NamedTuple

import jax
import jax.numpy as jnp
from jax import lax
from jax.experimental import pallas as pl
from jax.experimental.pallas import tpu as pltpu

F32 = jnp.float32
BF16 = jnp.bfloat16
I32 = jnp.int32

EPS = 1e-6
N_META = 16
HEAD = 128
TOPK_MAX = 256
LANE = 128
VMEM_LIMIT_BYTES = 56 * 2**20
MASKED = -1e30
INT_MIN = -2**31
KEY_NEG_INF = -2139095041

_NT = (((1,), (1,)), ((), ()))
_TN = (((0,), (0,)), ((), ()))


class Dims(NamedTuple):
    d: int
    t: int
    t_pad: int
    db: int
    ds: int
    r: int
    rp: int
    tm: int
    heads: int
    ih: int
    d_r: int
    q_rank: int
    d_in: int
    d_in_p: int
    d_ff: int
    d_ff_p: int
    n_pages: int
    page: int
    nsel_p: int
    nsel_s: int
    cb: int
    ch: int


def _round_up(x, m):
    return -(-x // m) * m


def _dims(x_prompt, x_sample, cache_k, page_table):
    d = x_prompt.shape[-1]
    seq = x_prompt.shape[1]
    db, ds = x_sample.shape[:2]
    t = seq + N_META
    t_pad = _round_up(t, LANE)
    nb = t_pad // LANE
    cb = 5 if nb % 5 == 0 else 1
    r = t + db * ds
    rp = _round_up(r, LANE)
    n_tiles = max(1, -(-rp // 1200))
    while (rp // 16) % n_tiles:
        n_tiles += 1
    heads = d // 256
    d_r = heads * HEAD
    q_rank = d // 4
    ih = d // 128
    d_in = 6 * d_r + q_rank + HEAD + ih
    d_ff = -(-8 * d // (3 * 256)) * 256
    n_pages, page = page_table.shape[1], cache_k.shape[2]
    past = n_pages * page
    return Dims(d=d, t=t, t_pad=t_pad, db=db, ds=ds, r=r, rp=rp, tm=rp // n_tiles, heads=heads, ih=ih,
                d_r=d_r, q_rank=q_rank, d_in=d_in, d_in_p=_round_up(d_in, 512), d_ff=d_ff,
                d_ff_p=_round_up(d_ff, 1024), n_pages=n_pages, page=page,
                nsel_p=min(TOPK_MAX, seq // 4), nsel_s=min(TOPK_MAX, (past + ds) // 4),
                cb=cb, ch=cb * LANE)


def _tile(n, pref):
    best = LANE
    for c in range(LANE, min(n, pref) + 1, LANE):
        if n % c == 0:
            best = c
    return best


def _params(*sem):
    return pltpu.CompilerParams(dimension_semantics=sem, vmem_limit_bytes=VMEM_LIMIT_BYTES)


def _sigmoid(x):
    return 1.0 / (1.0 + jnp.exp(-x))


def _silu(x):
    return x * _sigmoid(x)


def _rmsnorm_kernel(x_ref, g_ref, o_ref):
    x = x_ref[...]
    ms = jnp.mean(x * x, axis=-1, keepdims=True)
    o_ref[...] = (x * lax.rsqrt(ms + EPS) * g_ref[...]).astype(o_ref.dtype)


def _rmsnorm(x, g, *, width, col_block, out_dtype):
    rows = x.shape[0]
    tr = LANE
    return pl.pallas_call(
        _rmsnorm_kernel,
        grid=(rows // tr,),
        in_specs=[pl.BlockSpec((tr, width), lambda i: (i, col_block)),
                  pl.BlockSpec((1, width), lambda i: (0, 0))],
        out_specs=pl.BlockSpec((tr, width), lambda i: (i, 0)),
        out_shape=jax.ShapeDtypeStruct((rows, width), out_dtype),
        compiler_params=_params("parallel"),
        name="rmsnorm",
    )(x, g.reshape(1, width).astype(F32))


def _idx_prep_kernel(ik_ref, iw_ref, g_ref, b_ref, ki_ref, kib_ref, wt_ref, *, ih, scale):
    x = ik_ref[...]
    mu = jnp.mean(x, axis=-1, keepdims=True)
    xc = x - mu
    var = jnp.mean(xc * xc, axis=-1, keepdims=True)
    ki = xc * lax.rsqrt(var + EPS) * g_ref[...] + b_ref[...]
    ki_ref[...] = ki
    kib_ref[...] = ki.astype(BF16)
    wt_ref[...] = (iw_ref[...] * scale).T[:ih, :]


def _idx_prep(z, g, b, dm, off_ik):
    tr = LANE
    cb = off_ik // HEAD
    return pl.pallas_call(
        functools.partial(_idx_prep_kernel, ih=dm.ih, scale=dm.ih ** -0.5 * HEAD ** -0.5),
        grid=(dm.rp // tr,),
        in_specs=[pl.BlockSpec((tr, HEAD), lambda i: (i, cb)),
                  pl.BlockSpec((tr, HEAD), lambda i: (i, cb + 1)),
                  pl.BlockSpec((1, HEAD), lambda i: (0, 0)),
                  pl.BlockSpec((1, HEAD), lambda i: (0, 0))],
        out_specs=[pl.BlockSpec((tr, HEAD), lambda i: (i, 0)),
                   pl.BlockSpec((tr, HEAD), lambda i: (i, 0)),
                   pl.BlockSpec((dm.ih, tr), lambda i: (0, i))],
        out_shape=[jax.ShapeDtypeStruct((dm.rp, HEAD), F32),
                   jax.ShapeDtypeStruct((dm.rp, HEAD), BF16),
                   jax.ShapeDtypeStruct((dm.ih, dm.rp), F32)],
        compiler_params=_params("parallel"),
        name="idx_prep",
    )(z, z, g.reshape(1, HEAD), b.reshape(1, HEAD))


def _mm_kernel(a_ref, b_ref, o_ref, *, scale):
    acc = jnp.dot(a_ref[...], b_ref[...], preferred_element_type=F32)
    if scale != 1.0:
        acc = acc * scale
    o_ref[...] = acc.astype(o_ref.dtype)


def _mm_res_kernel(a_ref, b_ref, r_ref, o_ref):
    o_ref[...] = r_ref[...] + jnp.dot(a_ref[...], b_ref[...], preferred_element_type=F32)


def _matmul(a, b, dm, *, out_dtype, scale=1.0, residual=None, name):
    m, k = a.shape
    n = b.shape[1]
    tm, tn = dm.tm, _tile(n, 512)
    in_specs = [pl.BlockSpec((tm, k), lambda i, j: (i, 0)),
                pl.BlockSpec((k, tn), lambda i, j: (0, j))]
    args = [a, b]
    if residual is None:
        body = functools.partial(_mm_kernel, scale=scale)
    else:
        body = _mm_res_kernel
        in_specs.append(pl.BlockSpec((tm, tn), lambda i, j: (i, j)))
        args.append(residual)
    return pl.pallas_call(
        body,
        grid=(m // tm, n // tn),
        in_specs=in_specs,
        out_specs=pl.BlockSpec((tm, tn), lambda i, j: (i, j)),
        out_shape=jax.ShapeDtypeStruct((m, n), out_dtype),
        compiler_params=_params("parallel", "parallel"),
        name=name,
    )(*args)


def _ffn_up_kernel(x_ref, wg_ref, wu_ref, o_ref):
    x = x_ref[...]
    g = jnp.dot(x, wg_ref[...], preferred_element_type=F32)
    u = jnp.dot(x, wu_ref[...], preferred_element_type=F32)
    o_ref[...] = (_silu(g) * u).astype(o_ref.dtype)


def _ffn_up(x, wg, wu, dm):
    m, k = x.shape
    n = wg.shape[1]
    tm, tn = dm.tm, _tile(n, 512)
    return pl.pallas_call(
        _ffn_up_kernel,
        grid=(m // tm, n // tn),
        in_specs=[pl.BlockSpec((tm, k), lambda i, j: (i, 0)),
                  pl.BlockSpec((k, tn), lambda i, j: (0, j)),
                  pl.BlockSpec((k, tn), lambda i, j: (0, j))],
        out_specs=pl.BlockSpec((tm, tn), lambda i, j: (i, j)),
        out_shape=jax.ShapeDtypeStruct((m, n), BF16),
        compiler_params=_params("parallel", "parallel"),
        name="ffn_up",
    )(x, wg, wu)


def _ffn_down_kernel(a_ref, b_ref, r_ref, o_ref, acc_ref):
    kk = pl.program_id(2)

    @pl.when(kk == 0)
    def _():
        acc_ref[...] = r_ref[...]

    acc_ref[...] += jnp.dot(a_ref[...], b_ref[...], preferred_element_type=F32)

    @pl.when(kk == pl.num_programs(2) - 1)
    def _():
        o_ref[...] = acc_ref[...]


def _ffn_down(a, b, residual, dm):
    m, k = a.shape
    n = b.shape[1]
    tm, tn, tk = dm.tm, _tile(n, 512), _tile(k, 2816)
    return pl.pallas_call(
        _ffn_down_kernel,
        grid=(m // tm, n // tn, k // tk),
        in_specs=[pl.BlockSpec((tm, tk), lambda i, j, kk: (i, kk)),
                  pl.BlockSpec((tk, tn), lambda i, j, kk: (kk, j)),
                  pl.BlockSpec((tm, tn), lambda i, j, kk: (i, j))],
        out_specs=pl.BlockSpec((tm, tn), lambda i, j, kk: (i, j)),
        out_shape=jax.ShapeDtypeStruct((m, n), F32),
        scratch_shapes=[pltpu.VMEM((tm, tn), F32)],
        compiler_params=_params("parallel", "parallel", "arbitrary"),
        name="ffn_down",
    )(a, b, residual)


def _split3(x):
    hi = x.astype(BF16)
    r = x - hi.astype(F32)
    mid = r.astype(BF16)
    lo = (r - mid.astype(F32)).astype(BF16)
    return hi, mid, lo


def _dot3(a, x):
    hi, mid, lo = _split3(x)
    return (jnp.dot(a, hi, preferred_element_type=F32) + jnp.dot(a, mid, preferred_element_type=F32)
            + jnp.dot(a, lo, preferred_element_type=F32))


def _lower_bound(lbl):
    e = jnp.exp(lbl - jnp.max(lbl, axis=0, keepdims=True))
    return e[0:1, :] / jnp.sum(e, axis=0, keepdims=True)


def _hgrn_gates(q_r, f_r, lb):
    key = (1.0 - lb) * _sigmoid(-f_r)
    return _silu(q_r), key, jnp.log(1.0 - key)


def _diag_block(q, key, g, v, ones):
    n = q.shape[0]
    rows = lax.broadcasted_iota(I32, (n, HEAD), 0)
    parts = []
    for s in range(n):
        decay = jnp.exp(jnp.minimum(g - g[s:s + 1, :], 0.0))
        parts.append(jnp.where(rows >= s, q * (key[s:s + 1, :] * decay), 0.0))
    p = jnp.concatenate(parts, axis=0).astype(BF16)
    a = jnp.dot(p, ones, preferred_element_type=F32)
    o = a[0:n] * v[0:1, :]
    for s in range(1, n):
        o = o + a[s * n:(s + 1) * n] * v[s:s + 1, :]
    return o


def _hgrn_levels(q, key, g):
    c = q.shape[0]
    row = lax.broadcasted_iota(I32, (c, HEAD), 0)
    t_i = lax.broadcasted_iota(I32, (c, c), 0)
    s_i = lax.broadcasted_iota(I32, (c, c), 1)
    total = None
    b = c // 2
    while b >= 16:
        gref = g[b - 1:b, :]
        for m in range(2 * b, c, 2 * b):
            gref = jnp.where(row >= m, g[m + b - 1:m + b, :], gref)
        lower = (row & b) != 0
        qs = jnp.where(lower, q * jnp.exp(jnp.minimum(g - gref, 0.0)), 0.0)
        ks = jnp.where(lower, 0.0, key * jnp.exp(jnp.minimum(gref - g, 0.0)))
        a = lax.dot_general(qs.astype(BF16), ks.astype(BF16), _NT, preferred_element_type=F32)
        if 2 * b < c:
            a = jnp.where((t_i // (2 * b)) == (s_i // (2 * b)), a, 0.0)
        total = a if total is None else total + a
        b //= 2
    return total


def _scan_rows8(x):
    row = lax.broadcasted_iota(I32, x.shape, 0)
    for sh in (1, 2, 4):
        x = x + jnp.where(row >= sh, pltpu.roll(x, sh, 0), 0.0)
    return x


def _hgrn_out(o, g_norm, og):
    ms = jnp.mean(o * o, axis=-1, keepdims=True)
    return o * lax.rsqrt(ms + EPS) * g_norm * _silu(og)


def _hgrn_prompt_kernel(q_ref, f_ref, i_ref, og_ref, lbl_ref, gn_ref, o_ref, s_ref, st_ref, *, t):
    c = pl.program_id(1)

    @pl.when(c == 0)
    def _():
        st_ref[...] = jnp.zeros_like(st_ref)

    q, key, lf = _hgrn_gates(q_ref[...], f_ref[...], _lower_bound(lbl_ref[...]))
    v = i_ref[...]
    valid = (c * LANE + lax.broadcasted_iota(I32, (LANE, HEAD), 0)) < t
    key = jnp.where(valid, key, 0.0)
    lf = jnp.where(valid, lf, 0.0)

    tri = (lax.broadcasted_iota(I32, (LANE, LANE), 0) >= lax.broadcasted_iota(I32, (LANE, LANE), 1))
    g = _dot3(tri.astype(BF16), lf)
    g_last = g[LANE - 1:LANE, :]
    st = st_ref[...]
    vb = v.astype(BF16)
    o = lax.dot_general((q * jnp.exp(g)).astype(BF16), st.astype(BF16), _NT, preferred_element_type=F32)
    o = o + jnp.dot(_hgrn_levels(q, key, g).astype(BF16), vb, preferred_element_type=F32)
    ones = jnp.ones((HEAD, HEAD), BF16)
    o = o + jnp.concatenate(
        [_diag_block(q[m:m + 16], key[m:m + 16], g[m:m + 16], v[m:m + 16], ones) for m in range(0, LANE, 16)],
        axis=0)
    kd = (key * jnp.exp(g_last - g)).astype(BF16)
    st_new = jnp.exp(g_last) * st + lax.dot_general(vb, kd, _TN, preferred_element_type=F32)
    st_ref[...] = st_new
    o_ref[...] = _hgrn_out(o, gn_ref[...], og_ref[...]).astype(o_ref.dtype)

    @pl.when(c == pl.num_programs(1) - 1)
    def _():
        s_ref[...] = st_new.T


def _hgrn_prompt(z, lb_logits, g_norm, dm):
    nh, nc = dm.heads, dm.t_pad // LANE
    blk = lambda off: pl.BlockSpec((LANE, HEAD), lambda h, c, off=off: (c, off * nh + h))
    nl = lb_logits.shape[0]
    return pl.pallas_call(
        functools.partial(_hgrn_prompt_kernel, t=dm.t),
        grid=(nh, nc),
        in_specs=[blk(0), blk(1), blk(2), blk(3),
                  pl.BlockSpec((nl, HEAD), lambda h, c: (0, h)),
                  pl.BlockSpec((1, HEAD), lambda h, c: (0, 0))],
        out_specs=[pl.BlockSpec((LANE, HEAD), lambda h, c: (c, h)),
                   pl.BlockSpec((None, HEAD, HEAD), lambda h, c: (h, 0, 0))],
        out_shape=[jax.ShapeDtypeStruct((dm.t_pad, dm.d_r), BF16),
                   jax.ShapeDtypeStruct((nh, HEAD, HEAD), F32)],
        scratch_shapes=[pltpu.VMEM((HEAD, HEAD), F32)],
        compiler_params=_params("parallel", "arbitrary"),
        name="hgrn_prompt",
    )(z, z, z, z, lb_logits, g_norm.reshape(1, HEAD))


def _hgrn_sample_kernel(q_ref, f_ref, i_ref, og_ref, lbl_ref, gn_ref, s0_ref, o_ref, s_ref, *, nh, ds):
    q, key, lf = _hgrn_gates(q_ref[...], f_ref[...], _lower_bound(lbl_ref[...]))
    v = i_ref[...]
    g = _scan_rows8(lf)
    g_last = g[ds - 1:ds, :]
    qe = q * jnp.exp(g)
    kd = key * jnp.exp(g_last - g)
    e_last = jnp.exp(g_last)
    ones = jnp.ones((HEAD, HEAD), BF16)
    zpad = jnp.zeros((LANE - ds, HEAD), F32)
    og = og_ref[...]
    gn = gn_ref[...]
    for h in range(nh):
        sl = slice(h * HEAD, (h + 1) * HEAD)
        st = s0_ref[h].T
        o = lax.dot_general(qe[:, sl].astype(BF16), st.astype(BF16), _NT, preferred_element_type=F32)
        o = o + _diag_block(q[:, sl], key[:, sl], g[:, sl], v[:, sl], ones)
        vp = jnp.concatenate([v[:, sl], zpad], axis=0).astype(BF16)
        kp = jnp.concatenate([kd[:, sl], zpad], axis=0).astype(BF16)
        st_new = e_last[:, sl] * st + lax.dot_general(vp, kp, _TN, preferred_element_type=F32)
        s_ref[h] = st_new.T
        o_ref[:, sl] = _hgrn_out(o, gn, og[:, sl])


def _hgrn_sample(z, lb_logits, g_norm, state, dm):
    assert dm.ds == 8 and dm.t % dm.ds == 0
    nh, r0 = dm.heads, dm.t // dm.ds
    blk = lambda off: pl.BlockSpec((dm.ds, dm.d_r), lambda b, off=off: (r0 + b, off))
    nl = lb_logits.shape[0]
    return pl.pallas_call(
        functools.partial(_hgrn_sample_kernel, nh=nh, ds=dm.ds),
        grid=(dm.db,),
        in_specs=[blk(0), blk(1), blk(2), blk(3),
                  pl.BlockSpec((nl, dm.d_r), lambda b: (0, 0)),
                  pl.BlockSpec((1, HEAD), lambda b: (0, 0)),
                  pl.BlockSpec((None, nh, HEAD, HEAD), lambda b: (b, 0, 0, 0))],
        out_specs=[pl.BlockSpec((dm.ds, dm.d_r), lambda b: (b, 0)),
                   pl.BlockSpec((None, nh, HEAD, HEAD), lambda b: (b, 0, 0, 0))],
        out_shape=[jax.ShapeDtypeStruct((dm.db * dm.ds, dm.d_r), F32),
                   jax.ShapeDtypeStruct((dm.db, nh, HEAD, HEAD), F32)],
        compiler_params=_params("parallel"),
        name="hgrn_sample",
    )(z, z, z, z, lb_logits, g_norm.reshape(1, HEAD), state)


def _to_key(x):
    b = lax.bitcast_convert_type(x, I32)
    return b ^ (lax.shift_right_arithmetic(b, 31) & 0x7FFFFFFF)


def _kth_largest(count_ge, n_sel):
    def body(it, tp):
        bit = jnp.left_shift(jnp.int32(1), 31 - it)
        cand = tp | bit
        return jnp.where(count_ge(cand ^ INT_MIN) >= n_sel, cand, tp)
    return lax.fori_loop(0, 32, body, jnp.zeros((1, LANE), I32)) ^ INT_MIN


def _select_bias(keys, thr):
    return jnp.where((keys >= thr) & (keys > KEY_NEG_INF), 0.0, MASKED)


def _index_kernel(qi_ref, ki_ref, wt_ref, bias_ref, key_ref, *, ih, ch, cb, n_sel):
    i = pl.program_id(0)
    n_ch = (i + cb) // cb
    n_total = key_ref.shape[0] // ch
    q_pos = i * LANE + lax.broadcasted_iota(I32, (LANE, LANE), 1)
    k_off = lax.broadcasted_iota(I32, (LANE, LANE), 0)

    def score_chunk(c, carry):
        for sub in range(cb):
            r0 = pl.multiple_of(c * ch + sub * LANE, LANE)
            kc = ki_ref[pl.ds(r0, LANE), :]
            acc = jnp.zeros((LANE, LANE), F32)
            for hp in range(0, ih, 2):
                qh = jnp.concatenate([qi_ref[:, hp * HEAD:(hp + 1) * HEAD],
                                      qi_ref[:, (hp + 1) * HEAD:(hp + 2) * HEAD]], axis=0)
                s = lax.dot_general(kc, qh, _NT, preferred_element_type=F32)
                acc = acc + jnp.maximum(s[:, :LANE], 0.0) * wt_ref[pl.ds(hp, 1), :]
                acc = acc + jnp.maximum(s[:, LANE:], 0.0) * wt_ref[pl.ds(hp + 1, 1), :]
            keys = jnp.where(r0 + k_off <= q_pos, _to_key(acc), KEY_NEG_INF)
            key_ref[pl.ds(r0, LANE), :] = keys
        return carry

    lax.fori_loop(0, n_ch, score_chunk, 0)

    def count_ge(cand):
        def body(c, cnt):
            k = key_ref[pl.ds(pl.multiple_of(c * ch, ch), ch), :]
            return cnt + jnp.sum((k >= cand).astype(I32), axis=0, keepdims=True)
        return lax.fori_loop(0, n_ch, body, jnp.zeros((1, LANE), I32))

    thr = _kth_largest(count_ge, n_sel)

    def write_bias(c, carry):
        r0 = pl.multiple_of(c * ch, ch)
        bias_ref[pl.ds(r0, ch), :] = _select_bias(key_ref[pl.ds(r0, ch), :], thr).astype(bias_ref.dtype)
        return carry

    lax.fori_loop(0, n_ch, write_bias, 0)

    def write_masked(c, carry):
        bias_ref[pl.ds(pl.multiple_of(c * ch, ch), ch), :] = jnp.full((ch, LANE), MASKED, bias_ref.dtype)
        return carry

    lax.fori_loop(n_ch, n_total, write_masked, 0)


def _index_select_prompt(qi, ki_b, wt, dm):
    nb = dm.t_pad // LANE
    assert dm.ch >= dm.nsel_p and dm.ih % 2 == 0
    return pl.pallas_call(
        functools.partial(_index_kernel, ih=dm.ih, ch=dm.ch, cb=dm.cb, n_sel=dm.nsel_p),
        grid=(nb,),
        in_specs=[pl.BlockSpec((LANE, dm.ih * HEAD), lambda i: (i, 0)),
                  pl.BlockSpec((dm.t_pad, HEAD), lambda i: (0, 0)),
                  pl.BlockSpec((dm.ih, LANE), lambda i: (0, i))],
        out_specs=pl.BlockSpec((None, dm.t_pad, LANE), lambda i: (i, 0, 0)),
        out_shape=jax.ShapeDtypeStruct((nb, dm.t_pad, LANE), BF16),
        scratch_shapes=[pltpu.VMEM((dm.t_pad, LANE), I32)],
        compiler_params=_params("parallel"),
        name="index_select_prompt",
    )(qi, ki_b, wt)


def _attn_kernel(q_ref, k_ref, vt_ref, bias_ref, o_ref, s_ref, *, hpg, ch, cb):
    i = pl.program_id(1)
    n_ch = (i + cb) // cb
    qs = [q_ref[:, h * HEAD:(h + 1) * HEAD].astype(BF16) for h in range(hpg)]

    def scores(c, ms):
        r0 = pl.multiple_of(c * ch, ch)
        bias = bias_ref[pl.ds(r0, ch), :].astype(F32)
        out = []
        for h in range(hpg):
            s = lax.dot_general(k_ref[pl.ds(r0, ch), h * HEAD:(h + 1) * HEAD], qs[h], _NT,
                                preferred_element_type=F32) + bias
            s_ref[h, pl.ds(r0, ch), :] = s
            out.append(jnp.maximum(ms[h], jnp.max(s, axis=0, keepdims=True)))
        return tuple(out)

    ms = lax.fori_loop(0, n_ch, scores, tuple(jnp.full((1, LANE), -jnp.inf, F32) for _ in range(hpg)))

    def weighted(c, carry):
        r0 = pl.multiple_of(c * ch, ch)
        out = []
        for h in range(hpg):
            l, acc = carry[h]
            p = jnp.exp(s_ref[h, pl.ds(r0, ch), :] - ms[h])
            l = l + jnp.sum(p, axis=0, keepdims=True)
            acc = acc + jnp.dot(vt_ref[c, h * HEAD:(h + 1) * HEAD, :], p.astype(BF16),
                                preferred_element_type=F32)
            out.append((l, acc))
        return tuple(out)

    init = tuple((jnp.zeros((1, LANE), F32), jnp.zeros((HEAD, LANE), F32)) for _ in range(hpg))
    res = lax.fori_loop(0, n_ch, weighted, init)
    for h in range(hpg):
        l, acc = res[h]
        o_ref[:, h * HEAD:(h + 1) * HEAD] = (acc / l).T.astype(o_ref.dtype)


def _attend_prompt(qa, ka, vat, bias, dm):
    nb, hpg = dm.t_pad // LANE, 2
    ng = dm.heads // hpg
    w = hpg * HEAD
    return pl.pallas_call(
        functools.partial(_attn_kernel, hpg=hpg, ch=dm.ch, cb=dm.cb),
        grid=(ng, nb),
        in_specs=[pl.BlockSpec((LANE, w), lambda g, i: (i, g)),
                  pl.BlockSpec((dm.t_pad, w), lambda g, i: (0, g)),
                  pl.BlockSpec((dm.t_pad // dm.ch, w, dm.ch), lambda g, i: (0, g, 0)),
                  pl.BlockSpec((None, dm.t_pad, LANE), lambda g, i: (i, 0, 0))],
        out_specs=pl.BlockSpec((LANE, w), lambda g, i: (i, g)),
        out_shape=jax.ShapeDtypeStruct((dm.t_pad, dm.d_r), BF16),
        scratch_shapes=[pltpu.VMEM((hpg, dm.t_pad, LANE), F32)],
        compiler_params=_params("parallel", "parallel"),
        name="attend_prompt",
    )(qa, ka, vat, bias)


def _sample_index_kernel(pt_ref, *refs, n_pages, page, ih, ds, n_sel, scale):
    del pt_ref
    cik_refs = refs[:n_pages]
    qi_ref, kin_ref, w_ref, bias_ref, key_ref = refs[n_pages:]
    q2 = qi_ref[...]
    w = w_ref[...] * scale
    nq = ds * ih
    sel = (lax.broadcasted_iota(I32, (nq, LANE), 0) // ih
           == lax.broadcasted_iota(I32, (nq, LANE), 1) % ds).astype(BF16)
    lane_q = lax.broadcasted_iota(I32, (page, LANE), 1) % ds
    row = lax.broadcasted_iota(I32, (page, LANE), 0)
    for p in range(n_pages + 1):
        if p < n_pages:
            kp = cik_refs[p][...]
        else:
            kp = jnp.concatenate([kin_ref[...], jnp.zeros((page - ds, HEAD), F32)], axis=0)
        s = lax.dot_general(kp.astype(BF16), q2, _NT, preferred_element_type=F32)
        sc = _dot3_rhs(jnp.maximum(s, 0.0) * w, sel)
        keys = _to_key(sc)
        if p == n_pages:
            keys = jnp.where(row <= lane_q, keys, KEY_NEG_INF)
        key_ref[p * page:(p + 1) * page, :] = keys

    def count_ge(cand):
        return jnp.sum((key_ref[...] >= cand).astype(I32), axis=0, keepdims=True)

    thr = _kth_largest(count_ge, n_sel)
    bias_ref[...] = _select_bias(key_ref[...], thr).astype(bias_ref.dtype)


def _dot3_rhs(x, b):
    hi, mid, lo = _split3(x)
    return (jnp.dot(hi, b, preferred_element_type=F32) + jnp.dot(mid, b, preferred_element_type=F32)
            + jnp.dot(lo, b, preferred_element_type=F32))


def _index_select_sample(page_table, cik, qi, ki, w_s, dm):
    assert dm.page == LANE and dm.heads * dm.ds == LANE and dm.t % dm.ds == 0
    nq = dm.ds * dm.ih
    n_keys = (dm.n_pages + 1) * dm.page
    assert n_keys >= dm.nsel_s
    qi3 = qi.reshape(dm.rp * dm.ih, HEAD)
    r0 = dm.t // dm.ds
    page_spec = lambda p: pl.BlockSpec((None, dm.page, HEAD), lambda b, pt, p=p: (pt[b, p], 0, 0))
    grid_spec = pltpu.PrefetchScalarGridSpec(
        num_scalar_prefetch=1,
        grid=(dm.db,),
        in_specs=[page_spec(p) for p in range(dm.n_pages)] + [
            pl.BlockSpec((nq, HEAD), lambda b, pt: (r0 + b, 0)),
            pl.BlockSpec((dm.ds, HEAD), lambda b, pt: (r0 + b, 0)),
            pl.BlockSpec((None, 1, nq), lambda b, pt: (b, 0, 0))],
        out_specs=pl.BlockSpec((None, n_keys, LANE), lambda b, pt: (b, 0, 0)),
        scratch_shapes=[pltpu.VMEM((n_keys, LANE), I32)],
    )
    return pl.pallas_call(
        functools.partial(_sample_index_kernel, n_pages=dm.n_pages, page=dm.page, ih=dm.ih, ds=dm.ds,
                          n_sel=dm.nsel_s, scale=dm.ih ** -0.5 * HEAD ** -0.5),
        grid_spec=grid_spec,
        out_shape=jax.ShapeDtypeStruct((dm.db, n_keys, LANE), BF16),
        compiler_params=_params("parallel"),
        name="index_select_sample",
    )(page_table, *([cik] * dm.n_pages), qi3, ki, w_s)


def _sample_attn_kernel(pt_ref, *refs, pps, page, nh, ds):
    del pt_ref
    k_refs, v_refs = refs[:pps], refs[pps:2 * pps]
    q_ref, kn_ref, vn_ref, bias_ref, o_ref, s_ref, v_scr, qbd_ref = refs[2 * pps:]
    g = pl.program_id(1)
    w = nh * HEAD

    @pl.when(g == 0)
    def _():
        q8 = q_ref[...]
        tiled = jnp.concatenate([q8] * nh, axis=0)
        same = (lax.broadcasted_iota(I32, (LANE, w), 0) // ds == lax.broadcasted_iota(I32, (LANE, w), 1) // HEAD)
        qbd_ref[...] = jnp.where(same, tiled, 0.0).astype(BF16)

    qbd = qbd_ref[...]
    for p in range(pps):
        r0 = pl.multiple_of((g * pps + p) * page, page)
        s = lax.dot_general(k_refs[p][...].astype(BF16), qbd, _NT, preferred_element_type=F32)
        s_ref[pl.ds(r0, page), :] = s + bias_ref[pl.ds(r0, page), :].astype(F32)
        v_scr[pl.ds(r0, page), :] = v_refs[p][...].astype(BF16)

    @pl.when(g == pl.num_programs(1) - 1)
    def _():
        n_past = s_ref.shape[0] - page
        zpad = jnp.zeros((page - ds, w), F32)
        kn = jnp.concatenate([kn_ref[...], zpad], axis=0).astype(BF16)
        s_new = lax.dot_general(kn, qbd, _NT, preferred_element_type=F32)
        s_ref[n_past:, :] = s_new + bias_ref[n_past:, :].astype(F32)
        v_scr[n_past:, :] = jnp.concatenate([vn_ref[...], zpad], axis=0).astype(BF16)
        s = s_ref[...]
        p = jnp.exp(s - jnp.max(s, axis=0, keepdims=True))
        p = (p / jnp.sum(p, axis=0, keepdims=True)).astype(BF16)
        out = lax.dot_general(p, v_scr[...], _TN, preferred_element_type=F32)
        for h in range(nh):
            o_ref[:, h * HEAD:(h + 1) * HEAD] = out[h * ds:(h + 1) * ds, h * HEAD:(h + 1) * HEAD]


def _attend_sample(page_table, ck, cv, qa, k_new, v_new, bias, dm):
    pps = 4 if dm.n_pages % 4 == 0 else 1
    w = dm.heads * HEAD
    n_keys = (dm.n_pages + 1) * dm.page
    r0 = dm.t // dm.ds
    page_spec = lambda p: pl.BlockSpec((None, dm.page, w), lambda b, g, pt, p=p: (pt[b, g * pps + p], 0, 0))
    grid_spec = pltpu.PrefetchScalarGridSpec(
        num_scalar_prefetch=1,
        grid=(dm.db, dm.n_pages // pps),
        in_specs=[page_spec(p) for p in range(pps)] * 2 + [
            pl.BlockSpec((dm.ds, w), lambda b, g, pt: (r0 + b, 0)),
            pl.BlockSpec((dm.ds, w), lambda b, g, pt: (b, 0)),
            pl.BlockSpec((dm.ds, w), lambda b, g, pt: (b, 0)),
            pl.BlockSpec((None, n_keys, LANE), lambda b, g, pt: (b, 0, 0))],
        out_specs=pl.BlockSpec((dm.ds, w), lambda b, g, pt: (b, 0)),
        scratch_shapes=[pltpu.VMEM((n_keys, LANE), F32), pltpu.VMEM((n_keys, w), BF16),
                        pltpu.VMEM((LANE, w), BF16)],
    )
    return pl.pallas_call(
        functools.partial(_sample_attn_kernel, pps=pps, page=dm.page, nh=dm.heads, ds=dm.ds),
        grid_spec=grid_spec,
        out_shape=jax.ShapeDtypeStruct((dm.db * dm.ds, w), F32),
        compiler_params=_params("parallel", "arbitrary"),
        name="attend_sample",
    )(page_table, *([ck] * pps), *([cv] * pps), qa, k_new, v_new, bias)


def _bf16_padded(w, rows, cols):
    w = w.astype(BF16)
    return jnp.pad(w, ((0, rows - w.shape[0]), (0, cols - w.shape[1])))


def kernel(x_prompt, x_sample, cache_k, cache_v, cache_idx_k, state_hgrn, page_table, meta_tokens, lb_logits,
           norm_mix_g, w_in, q_norm_g, w_uq, w_uq_idx, idx_k_norm_g, idx_k_norm_b, hgrn_norm_g, w_out,
           norm_ffn_g, w_gate, w_up, w_down, final_norm_g):
    assert x_prompt.shape[0] == 1 and w_in.shape[0] == 1, "single prompt sequence, single layer"
    dm = _dims(x_prompt, x_sample, cache_k, page_table)
    d, t, r, rp = dm.d, dm.t, dm.r, dm.rp
    ns = dm.db * dm.ds
    off_cq = 4 * dm.d_r
    off_k = off_cq + dm.q_rank
    off_v = off_k + dm.d_r
    off_ik = off_v + dm.d_r

    h0 = jnp.concatenate([meta_tokens.astype(F32), x_prompt[0], x_sample.reshape(ns, d),
                          jnp.zeros((rp - r, d), F32)], axis=0)
    hn = _rmsnorm(h0, norm_mix_g[0], width=d, col_block=0, out_dtype=BF16)
    z = _matmul(hn, _bf16_padded(w_in[0], d, dm.d_in_p), dm, out_dtype=F32, name="in_proj")

    k_all = z[:r, off_k:off_k + dm.d_r]
    v_all = z[:r, off_v:off_v + dm.d_r]

    o_r_p, s_p = _hgrn_prompt(z, lb_logits, hgrn_norm_g[0], dm)
    o_r_s, s_s = _hgrn_sample(z, lb_logits, hgrn_norm_g[0], state_hgrn[0], dm)

    cqn = _rmsnorm(z, q_norm_g[0], width=dm.q_rank, col_block=off_cq // dm.q_rank, out_dtype=BF16)
    qa = _matmul(cqn, w_uq[0].astype(BF16), dm, out_dtype=F32, scale=HEAD ** -0.5, name="q_up")
    qi = _matmul(cqn, w_uq_idx[0].astype(BF16), dm, out_dtype=BF16, name="q_idx_up")
    ki, ki_b, wt = _idx_prep(z, idx_k_norm_g[0], idx_k_norm_b[0], dm, off_ik)

    ka = z[:dm.t_pad, off_k:off_k + dm.d_r].astype(BF16)
    nc = dm.t_pad // dm.ch
    vat = z[:dm.t_pad, off_v:off_v + dm.d_r].astype(BF16).T.reshape(dm.d_r, nc, dm.ch).transpose(1, 0, 2)
    bias_p = _index_select_prompt(qi, ki_b, wt, dm)
    o_a_p = _attend_prompt(qa, ka, vat, bias_p, dm)

    iw_off = off_ik + HEAD
    w_s = z[t:r, iw_off:iw_off + dm.ih].reshape(dm.db, 1, dm.ds * dm.ih)
    bias_s = _index_select_sample(page_table, cache_idx_k[0], qi, ki, w_s, dm)
    n_pool = cache_k.shape[1]
    o_a_s = _attend_sample(page_table, cache_k[0].reshape(n_pool, dm.page, dm.d_r),
                           cache_v[0].reshape(n_pool, dm.page, dm.d_r), qa, k_all[t:], v_all[t:], bias_s, dm)

    o_mix = jnp.concatenate([
        jnp.concatenate([o_r_p[:t], o_a_p[:t]], axis=1),
        jnp.concatenate([o_r_s, o_a_s], axis=1).astype(BF16),
        jnp.zeros((rp - r, 2 * dm.d_r), BF16)], axis=0)
    h1 = _matmul(o_mix, w_out[0].astype(BF16), dm, out_dtype=F32, residual=h0, name="out_proj")

    hf = _rmsnorm(h1, norm_ffn_g[0], width=d, col_block=0, out_dtype=BF16)
    act = _ffn_up(hf, _bf16_padded(w_gate[0], d, dm.d_ff_p), _bf16_padded(w_up[0], d, dm.d_ff_p), dm)
    h2 = _ffn_down(act, _bf16_padded(w_down[0], dm.d_ff_p, d), h1, dm)
    y = _rmsnorm(h2, final_norm_g, width=d, col_block=0, out_dtype=F32)

    prompt_heads = lambda a: a[:t].reshape(1, 1, t, dm.heads, HEAD)
    sample_heads = lambda a: a[t:].reshape(1, dm.db, dm.ds, dm.heads, HEAD)
    return (y[N_META:t][None],
            y[t:r].reshape(dm.db, dm.ds, d),
            prompt_heads(k_all), prompt_heads(v_all), ki[:t].reshape(1, 1, t, HEAD), s_p[None, None],
            sample_heads(k_all), sample_heads(v_all), ki[t:r].reshape(1, dm.db, dm.ds, HEAD), s_s[None])
```

```python
import functools
from typing import NamedTuple

import jax
import jax.numpy as jnp
from jax import lax
from jax.experimental import pallas as pl
from jax.experimental.pallas import tpu as pltpu

F32 = jnp.float32
BF16 = jnp.bfloat16
I32 = jnp.int32

EPS = 1e-6
N_META = 16
HEAD = 128
TOPK_MAX = 256
LANE = 128
VMEM_LIMIT_BYTES = 56 * 2**20
QUERY_BLOCKS = 2
MASKED = -1e30
INT_MIN = -2**31
KEY_NEG_INF = -2139095041

_NT = (((1,), (1,)), ((), ()))
_TN = (((0,), (0,)), ((), ()))


class Dims(NamedTuple):
    d: int
    t: int
    t_pad: int
    tq_pad: int
    db: int
    ds: int
    r: int
    rp: int
    tm: int
    heads: int
    ih: int
    d_r: int
    q_rank: int
    d_in: int
    d_in_p: int
    d_ff: int
    n_pages: int
    page: int
    nsel_p: int
    nsel_s: int
    cb: int
    ch: int


def _round_up(x, m):
    return -(-x // m) * m


def _dims(x_prompt, x_sample, cache_k, page_table):
    d = x_prompt.shape[-1]
    seq = x_prompt.shape[1]
    db, ds = x_sample.shape[:2]
    t = seq + N_META
    t_pad = _round_up(t, LANE)
    nb = t_pad // LANE
    cb = 5 if nb % 5 == 0 else 1
    tq_pad = _round_up(t, QUERY_BLOCKS * LANE)
    r = t + db * ds
    rp = max(_round_up(r, LANE), tq_pad)
    n_tiles = max(1, -(-rp // 1200))
    while (rp // 16) % n_tiles:
        n_tiles += 1
    heads = d // 256
    d_r = heads * HEAD
    q_rank = d // 4
    ih = d // 128
    d_in = 6 * d_r + q_rank + HEAD + ih
    d_ff = -(-8 * d // (3 * 256)) * 256
    n_pages, page = page_table.shape[1], cache_k.shape[2]
    past = n_pages * page
    return Dims(d=d, t=t, t_pad=t_pad, tq_pad=tq_pad, db=db, ds=ds, r=r, rp=rp, tm=rp // n_tiles, heads=heads, ih=ih,
                d_r=d_r, q_rank=q_rank, d_in=d_in, d_in_p=_round_up(d_in, 512), d_ff=d_ff,
                n_pages=n_pages, page=page,
                nsel_p=min(TOPK_MAX, seq // 4), nsel_s=min(TOPK_MAX, (past + ds) // 4),
                cb=cb, ch=cb * LANE)


def _tile(n, pref):
    best = LANE
    for c in range(LANE, min(n, pref) + 1, LANE):
        if n % c == 0:
            best = c
    return best


def _params(*sem):
    return pltpu.CompilerParams(dimension_semantics=sem, vmem_limit_bytes=VMEM_LIMIT_BYTES)


def _sigmoid(x):
    return 1.0 / (1.0 + jnp.exp(-x))


def _silu(x):
    return x * _sigmoid(x)


def _rmsnorm_kernel(x_ref, g_ref, o_ref):
    x = x_ref[...]
    ms = jnp.mean(x * x, axis=-1, keepdims=True)
    o_ref[...] = (x * lax.rsqrt(ms + EPS) * g_ref[...]).astype(o_ref.dtype)


def _rmsnorm(x, g, *, width, col_block, out_dtype):
    rows = x.shape[0]
    tr = LANE
    return pl.pallas_call(
        _rmsnorm_kernel,
        grid=(rows // tr,),
        in_specs=[pl.BlockSpec((tr, width), lambda i: (i, col_block)),
                  pl.BlockSpec((1, width), lambda i: (0, 0))],
        out_specs=pl.BlockSpec((tr, width), lambda i: (i, 0)),
        out_shape=jax.ShapeDtypeStruct((rows, width), out_dtype),
        compiler_params=_params("parallel"),
        name="rmsnorm",
    )(x, g.reshape(1, width).astype(F32))


def _idx_prep_kernel(ik_ref, iw_ref, g_ref, b_ref, ki_ref, kib_ref, wt_ref, *, ih, scale):
    x = ik_ref[...]
    mu = jnp.mean(x, axis=-1, keepdims=True)
    xc = x - mu
    var = jnp.mean(xc * xc, axis=-1, keepdims=True)
    ki = xc * lax.rsqrt(var + EPS) * g_ref[...] + b_ref[...]
    ki_ref[...] = ki
    kib_ref[...] = ki.astype(BF16)
    wt_ref[...] = (iw_ref[...] * scale).T[:ih, :]


def _idx_prep(z, g, b, dm, off_ik):
    tr = LANE
    cb = off_ik // HEAD
    return pl.pallas_call(
        functools.partial(_idx_prep_kernel, ih=dm.ih, scale=dm.ih ** -0.5 * HEAD ** -0.5),
        grid=(dm.rp // tr,),
        in_specs=[pl.BlockSpec((tr, HEAD), lambda i: (i, cb)),
                  pl.BlockSpec((tr, HEAD), lambda i: (i, cb + 1)),
                  pl.BlockSpec((1, HEAD), lambda i: (0, 0)),
                  pl.BlockSpec((1, HEAD), lambda i: (0, 0))],
        out_specs=[pl.BlockSpec((tr, HEAD), lambda i: (i, 0)),
                   pl.BlockSpec((tr, HEAD), lambda i: (i, 0)),
                   pl.BlockSpec((dm.ih, tr), lambda i: (0, i))],
        out_shape=[jax.ShapeDtypeStruct((dm.rp, HEAD), F32),
                   jax.ShapeDtypeStruct((dm.rp, HEAD), BF16),
                   jax.ShapeDtypeStruct((dm.ih, dm.rp), F32)],
        compiler_params=_params("parallel"),
        name="idx_prep",
    )(z, z, g.reshape(1, HEAD), b.reshape(1, HEAD))


def _mm_kernel(a_ref, b_ref, o_ref, *, scale):
    acc = jnp.dot(a_ref[...], b_ref[...], preferred_element_type=F32)
    if scale != 1.0:
        acc = acc * scale
    o_ref[...] = acc.astype(o_ref.dtype)


def _mm_res_kernel(a_ref, b_ref, r_ref, o_ref):
    o_ref[...] = r_ref[...] + jnp.dot(a_ref[...], b_ref[...], preferred_element_type=F32)


def _matmul(a, b, dm, *, out_dtype, scale=1.0, residual=None, name):
    m, k = a.shape
    n = b.shape[1]
    tm, tn = dm.tm, _tile(n, 512)
    in_specs = [pl.BlockSpec((tm, k), lambda i, j: (i, 0)),
                pl.BlockSpec((k, tn), lambda i, j: (0, j))]
    args = [a, b]
    if residual is None:
        body = functools.partial(_mm_kernel, scale=scale)
    else:
        body = _mm_res_kernel
        in_specs.append(pl.BlockSpec((tm, tn), lambda i, j: (i, j)))
        args.append(residual)
    return pl.pallas_call(
        body,
        grid=(m // tm, n // tn),
        in_specs=in_specs,
        out_specs=pl.BlockSpec((tm, tn), lambda i, j: (i, j)),
        out_shape=jax.ShapeDtypeStruct((m, n), out_dtype),
        compiler_params=_params("parallel", "parallel"),
        name=name,
    )(*args)


def _ffn_up_kernel(x_ref, wg_ref, wu_ref, o_ref):
    x = x_ref[...]
    g = jnp.dot(x, wg_ref[...], preferred_element_type=F32)
    u = jnp.dot(x, wu_ref[...], preferred_element_type=F32)
    o_ref[...] = (_silu(g) * u).astype(o_ref.dtype)


def _ffn_up(x, wg, wu, dm):
    m, k = x.shape
    n = wg.shape[1]
    tm, tn = dm.tm, _tile(n, 512)
    return pl.pallas_call(
        _ffn_up_kernel,
        grid=(m // tm, n // tn),
        in_specs=[pl.BlockSpec((tm, k), lambda i, j: (i, 0)),
                  pl.BlockSpec((k, tn), lambda i, j: (0, j)),
                  pl.BlockSpec((k, tn), lambda i, j: (0, j))],
        out_specs=pl.BlockSpec((tm, tn), lambda i, j: (i, j)),
        out_shape=jax.ShapeDtypeStruct((m, n), BF16),
        compiler_params=_params("parallel", "parallel"),
        name="ffn_up",
    )(x, wg, wu)


def _ffn_down_kernel(a_ref, b_ref, r_ref, o_ref, acc_ref):
    kk = pl.program_id(2)

    @pl.when(kk == 0)
    def _():
        acc_ref[...] = r_ref[...]

    acc_ref[...] += jnp.dot(a_ref[...], b_ref[...], preferred_element_type=F32)

    @pl.when(kk == pl.num_programs(2) - 1)
    def _():
        o_ref[...] = acc_ref[...]


def _ffn_down(a, b, residual, dm):
    m, k = a.shape
    n = b.shape[1]
    tm, tn, tk = dm.tm, _tile(n, 256), _tile(k, 5504)
    return pl.pallas_call(
        _ffn_down_kernel,
        grid=(m // tm, n // tn, k // tk),
        in_specs=[pl.BlockSpec((tm, tk), lambda i, j, kk: (i, kk)),
                  pl.BlockSpec((tk, tn), lambda i, j, kk: (kk, j)),
                  pl.BlockSpec((tm, tn), lambda i, j, kk: (i, j))],
        out_specs=pl.BlockSpec((tm, tn), lambda i, j, kk: (i, j)),
        out_shape=jax.ShapeDtypeStruct((m, n), F32),
        scratch_shapes=[pltpu.VMEM((tm, tn), F32)],
        compiler_params=_params("parallel", "parallel", "arbitrary"),
        name="ffn_down",
    )(a, b, residual)


def _split3(x):
    hi = x.astype(BF16)
    r = x - hi.astype(F32)
    mid = r.astype(BF16)
    lo = (r - mid.astype(F32)).astype(BF16)
    return hi, mid, lo


def _dot3(a, x):
    hi, mid, lo = _split3(x)
    return (jnp.dot(a, hi, preferred_element_type=F32) + jnp.dot(a, mid, preferred_element_type=F32)
            + jnp.dot(a, lo, preferred_element_type=F32))


def _lower_bound(lbl):
    e = jnp.exp(lbl - jnp.max(lbl, axis=0, keepdims=True))
    return e[0:1, :] / jnp.sum(e, axis=0, keepdims=True)


def _hgrn_gates(q_r, f_r, lb):
    key = (1.0 - lb) * _sigmoid(-f_r)
    return _silu(q_r), key, jnp.log(1.0 - key)


def _diag_block(q, key, g, v, ones):
    n = q.shape[0]
    rows = lax.broadcasted_iota(I32, (n, HEAD), 0)
    parts = []
    for s in range(n):
        decay = jnp.exp(jnp.minimum(g - g[s:s + 1, :], 0.0))
        parts.append(jnp.where(rows >= s, q * (key[s:s + 1, :] * decay), 0.0))
    p = jnp.concatenate(parts, axis=0).astype(BF16)
    a = jnp.dot(p, ones, preferred_element_type=F32)
    o = a[0:n] * v[0:1, :]
    for s in range(1, n):
        o = o + a[s * n:(s + 1) * n] * v[s:s + 1, :]
    return o


def _hgrn_levels(q, key, g):
    c = q.shape[0]
    row = lax.broadcasted_iota(I32, (c, HEAD), 0)
    t_i = lax.broadcasted_iota(I32, (c, c), 0)
    s_i = lax.broadcasted_iota(I32, (c, c), 1)
    total = None
    b = c // 2
    while b >= 16:
        gref = g[b - 1:b, :]
        for m in range(2 * b, c, 2 * b):
            gref = jnp.where(row >= m, g[m + b - 1:m + b, :], gref)
        lower = (row & b) != 0
        qs = jnp.where(lower, q * jnp.exp(jnp.minimum(g - gref, 0.0)), 0.0)
        ks = jnp.where(lower, 0.0, key * jnp.exp(jnp.minimum(gref - g, 0.0)))
        a = lax.dot_general(qs.astype(BF16), ks.astype(BF16), _NT, preferred_element_type=F32)
        if 2 * b < c:
            a = jnp.where((t_i // (2 * b)) == (s_i // (2 * b)), a, 0.0)
        total = a if total is None else total + a
        b //= 2
    return total


def _scan_rows8(x):
    row = lax.broadcasted_iota(I32, x.shape, 0)
    for sh in (1, 2, 4):
        x = x + jnp.where(row >= sh, pltpu.roll(x, sh, 0), 0.0)
    return x


def _hgrn_out(o, g_norm, og):
    ms = jnp.mean(o * o, axis=-1, keepdims=True)
    return o * lax.rsqrt(ms + EPS) * g_norm * _silu(og)


def _hgrn_prompt_kernel(q_ref, f_ref, i_ref, og_ref, lbl_ref, gn_ref, o_ref, s_ref, st_ref, *, t):
    c = pl.program_id(1)

    @pl.when(c == 0)
    def _():
        st_ref[...] = jnp.zeros_like(st_ref)

    q, key, lf = _hgrn_gates(q_ref[...], f_ref[...], _lower_bound(lbl_ref[...]))
    v = i_ref[...]
    valid = (c * LANE + lax.broadcasted_iota(I32, (LANE, HEAD), 0)) < t
    key = jnp.where(valid, key, 0.0)
    lf = jnp.where(valid, lf, 0.0)

    tri = (lax.broadcasted_iota(I32, (LANE, LANE), 0) >= lax.broadcasted_iota(I32, (LANE, LANE), 1))
    g = _dot3(tri.astype(BF16), lf)
    g_last = g[LANE - 1:LANE, :]
    st = st_ref[...]
    vb = v.astype(BF16)
    o = lax.dot_general((q * jnp.exp(g)).astype(BF16), st.astype(BF16), _NT, preferred_element_type=F32)
    o = o + jnp.dot(_hgrn_levels(q, key, g).astype(BF16), vb, preferred_element_type=F32)
    ones = jnp.ones((HEAD, HEAD), BF16)
    o = o + jnp.concatenate(
        [_diag_block(q[m:m + 16], key[m:m + 16], g[m:m + 16], v[m:m + 16], ones) for m in range(0, LANE, 16)],
        axis=0)
    kd = (key * jnp.exp(g_last - g)).astype(BF16)
    st_new = jnp.exp(g_last) * st + lax.dot_general(vb, kd, _TN, preferred_element_type=F32)
    st_ref[...] = st_new
    o_ref[...] = _hgrn_out(o, gn_ref[...], og_ref[...]).astype(o_ref.dtype)

    @pl.when(c == pl.num_programs(1) - 1)
    def _():
        s_ref[...] = st_new.T


def _hgrn_prompt(z, lb_logits, g_norm, dm):
    nh, nc = dm.heads, dm.t_pad // LANE
    blk = lambda off: pl.BlockSpec((LANE, HEAD), lambda h, c, off=off: (c, off * nh + h))
    nl = lb_logits.shape[0]
    return pl.pallas_call(
        functools.partial(_hgrn_prompt_kernel, t=dm.t),
        grid=(nh, nc),
        in_specs=[blk(0), blk(1), blk(2), blk(3),
                  pl.BlockSpec((nl, HEAD), lambda h, c: (0, h)),
                  pl.BlockSpec((1, HEAD), lambda h, c: (0, 0))],
        out_specs=[pl.BlockSpec((LANE, HEAD), lambda h, c: (c, h)),
                   pl.BlockSpec((None, HEAD, HEAD), lambda h, c: (h, 0, 0))],
        out_shape=[jax.ShapeDtypeStruct((dm.t_pad, dm.d_r), BF16),
                   jax.ShapeDtypeStruct((nh, HEAD, HEAD), F32)],
        scratch_shapes=[pltpu.VMEM((HEAD, HEAD), F32)],
        compiler_params=_params("parallel", "arbitrary"),
        name="hgrn_prompt",
    )(z, z, z, z, lb_logits, g_norm.reshape(1, HEAD))


def _hgrn_sample_kernel(q_ref, f_ref, i_ref, og_ref, lbl_ref, gn_ref, s0_ref, o_ref, s_ref, *, nh, ds):
    q, key, lf = _hgrn_gates(q_ref[...], f_ref[...], _lower_bound(lbl_ref[...]))
    v = i_ref[...]
    g = _scan_rows8(lf)
    g_last = g[ds - 1:ds, :]
    qe = q * jnp.exp(g)
    kd = key * jnp.exp(g_last - g)
    e_last = jnp.exp(g_last)
    ones = jnp.ones((HEAD, HEAD), BF16)
    zpad = jnp.zeros((LANE - ds, HEAD), F32)
    og = og_ref[...]
    gn = gn_ref[...]
    for h in range(nh):
        sl = slice(h * HEAD, (h + 1) * HEAD)
        st = s0_ref[h].T
        o = lax.dot_general(qe[:, sl].astype(BF16), st.astype(BF16), _NT, preferred_element_type=F32)
        o = o + _diag_block(q[:, sl], key[:, sl], g[:, sl], v[:, sl], ones)
        vp = jnp.concatenate([v[:, sl], zpad], axis=0).astype(BF16)
        kp = jnp.concatenate([kd[:, sl], zpad], axis=0).astype(BF16)
        st_new = e_last[:, sl] * st + lax.dot_general(vp, kp, _TN, preferred_element_type=F32)
        s_ref[h] = st_new.T
        o_ref[:, sl] = _hgrn_out(o, gn, og[:, sl])


def _hgrn_sample(z, lb_logits, g_norm, state, dm):
    assert dm.ds == 8 and dm.t % dm.ds == 0
    nh, r0 = dm.heads, dm.t // dm.ds
    blk = lambda off: pl.BlockSpec((dm.ds, dm.d_r), lambda b, off=off: (r0 + b, off))
    nl = lb_logits.shape[0]
    return pl.pallas_call(
        functools.partial(_hgrn_sample_kernel, nh=nh, ds=dm.ds),
        grid=(dm.db,),
        in_specs=[blk(0), blk(1), blk(2), blk(3),
                  pl.BlockSpec((nl, dm.d_r), lambda b: (0, 0)),
                  pl.BlockSpec((1, HEAD), lambda b: (0, 0)),
                  pl.BlockSpec((None, nh, HEAD, HEAD), lambda b: (b, 0, 0, 0))],
        out_specs=[pl.BlockSpec((dm.ds, dm.d_r), lambda b: (b, 0)),
                   pl.BlockSpec((None, nh, HEAD, HEAD), lambda b: (b, 0, 0, 0))],
        out_shape=[jax.ShapeDtypeStruct((dm.db * dm.ds, dm.d_r), F32),
                   jax.ShapeDtypeStruct((dm.db, nh, HEAD, HEAD), F32)],
        compiler_params=_params("parallel"),
        name="hgrn_sample",
    )(z, z, z, z, lb_logits, g_norm.reshape(1, HEAD), state)


def _to_key(x):
    b = lax.bitcast_convert_type(x, I32)
    return b ^ (lax.shift_right_arithmetic(b, 31) & 0x7FFFFFFF)


def _fold_rows(x, op):
    total = None
    for r0 in range(0, x.shape[0], 256):
        parts = [x[r:r + 8] for r in range(r0, min(r0 + 256, x.shape[0]), 8)]
        while len(parts) > 1:
            parts = [op(parts[j], parts[j + 1]) for j in range(0, len(parts) - 1, 2)] + parts[len(parts) & ~1:]
        total = parts[0] if total is None else op(total, parts[0])
    return total


def _count_ge(keys, cand):
    return _fold_rows((keys >= cand).astype(I32), jnp.add)


def _kth_largest(count_ge, n_sel):
    def body(it, tp):
        bit = jnp.left_shift(jnp.int32(1), 31 - it)
        cand = tp | bit
        cnt = jnp.sum(count_ge(cand ^ INT_MIN), axis=0, keepdims=True)
        return jnp.where(cnt >= n_sel, cand, tp)
    return lax.fori_loop(0, 32, body, jnp.zeros((1, LANE), I32)) ^ INT_MIN


def _kth_largest_replicated(count_ge, n_sel, period):
    copies = LANE // period
    bits = copies.bit_length() - 1
    assert copies == 1 << bits and 32 % bits == 0
    digit = lax.broadcasted_iota(I32, (8, LANE), 1) // period

    def body(it, tp):
        shift = 32 - bits * (it + 1)
        cand = tp | jnp.left_shift(digit, shift)
        cnt = jnp.sum(count_ge(cand[0:1] ^ INT_MIN), axis=0, keepdims=True)
        ok = jnp.broadcast_to((cnt >= n_sel).astype(I32), (8, LANE))
        step = period
        while step < LANE:
            ok = ok + pltpu.roll(ok, step, 1)
            step *= 2
        return tp | jnp.left_shift(ok - 1, shift)

    tp = lax.fori_loop(0, 32 // bits, body, jnp.zeros((8, LANE), I32))
    return tp[0:1] ^ INT_MIN


def _select_bias(keys, thr):
    return jnp.where((keys >= thr) & (keys > KEY_NEG_INF), 0.0, MASKED)


def _index_kernel(qi_ref, ki_ref, wt_ref, bias_ref, key_ref, qt_ref, *, ih, ch, cb, n_sel):
    i = pl.program_id(0)
    n_tiles = jnp.minimum(i + 1, key_ref.shape[0] // LANE)
    n_ch = (n_tiles + cb - 1) // cb
    n_total = key_ref.shape[0] // ch
    for hp in range(ih // 2):
        pair = [qi_ref[:, (2 * hp + e) * HEAD:(2 * hp + e + 1) * HEAD].astype(F32).T for e in (0, 1)]
        qt_ref[hp] = jnp.concatenate(pair, axis=1).astype(BF16)
    q_pos = i * LANE + lax.broadcasted_iota(I32, (LANE, LANE), 1)
    k_off = lax.broadcasted_iota(I32, (LANE, LANE), 0)

    def score_tile(c, carry):
        r0 = pl.multiple_of(c * LANE, LANE)
        kc = ki_ref[pl.ds(r0, LANE), :]
        acc = jnp.zeros((LANE, LANE), F32)
        for hp in range(ih // 2):
            s = jnp.dot(kc, qt_ref[hp], preferred_element_type=F32)
            acc = acc + jnp.maximum(s[:, :LANE], 0.0) * wt_ref[pl.ds(2 * hp, 1), :]
            acc = acc + jnp.maximum(s[:, LANE:], 0.0) * wt_ref[pl.ds(2 * hp + 1, 1), :]
        key_ref[pl.ds(r0, LANE), :] = jnp.where(r0 + k_off <= q_pos, _to_key(acc), KEY_NEG_INF)
        return carry

    lax.fori_loop(0, n_tiles, score_tile, 0)

    def fill_tile(c, carry):
        key_ref[pl.ds(pl.multiple_of(c * LANE, LANE), LANE), :] = jnp.full((LANE, LANE), KEY_NEG_INF, I32)
        return carry

    lax.fori_loop(n_tiles, n_ch * cb, fill_tile, 0)

    def count_ge(cand):
        def body(c, cnt):
            return cnt + _count_ge(key_ref[pl.ds(pl.multiple_of(c * ch, ch), ch), :], cand)
        return lax.fori_loop(0, n_ch, body, jnp.zeros((8, LANE), I32))

    thr = _kth_largest(count_ge, n_sel)

    def write_bias(c, carry):
        r0 = pl.multiple_of(c * ch, ch)
        bias_ref[pl.ds(r0, ch), :] = _select_bias(key_ref[pl.ds(r0, ch), :], thr).astype(bias_ref.dtype)
        return carry

    lax.fori_loop(0, n_ch, write_bias, 0)

    def write_masked(c, carry):
        bias_ref[pl.ds(pl.multiple_of(c * ch, ch), ch), :] = jnp.full((ch, LANE), MASKED, bias_ref.dtype)
        return carry

    lax.fori_loop(n_ch, n_total, write_masked, 0)


def _index_select_prompt(qi, ki_b, wt, dm):
    nb = dm.tq_pad // LANE
    assert dm.ch >= dm.nsel_p and dm.ih % 2 == 0
    return pl.pallas_call(
        functools.partial(_index_kernel, ih=dm.ih, ch=dm.ch, cb=dm.cb, n_sel=dm.nsel_p),
        grid=(nb,),
        in_specs=[pl.BlockSpec((LANE, dm.ih * HEAD), lambda i: (i, 0)),
                  pl.BlockSpec((dm.t_pad, HEAD), lambda i: (0, 0)),
                  pl.BlockSpec((dm.ih, LANE), lambda i: (0, i))],
        out_specs=pl.BlockSpec((None, dm.t_pad, LANE), lambda i: (i, 0, 0)),
        out_shape=jax.ShapeDtypeStruct((nb, dm.t_pad, LANE), BF16),
        scratch_shapes=[pltpu.VMEM((dm.t_pad, LANE), I32), pltpu.VMEM((dm.ih // 2, HEAD, 2 * LANE), BF16)],
        compiler_params=_params("parallel"),
        name="index_select_prompt",
    )(qi, ki_b, wt)


def _attn_kernel(q_ref, k_ref, vt_ref, bias_ref, o_ref, s_ref, *, hpg, qb, ch, cb):
    i = pl.program_id(1)
    n_tiles = jnp.minimum(qb * (i + 1), k_ref.shape[0] // LANE)
    n_ch = (n_tiles + cb - 1) // cb
    nq = qb * LANE
    qts = [jnp.concatenate([q_ref[a * LANE:(a + 1) * LANE, h * HEAD:(h + 1) * HEAD].T for a in range(qb)],
                           axis=1).astype(BF16) for h in range(hpg)]

    def scores(c, ms):
        r0 = pl.multiple_of(c * ch, ch)
        bias = jnp.concatenate([bias_ref[a, pl.ds(r0, ch), :] for a in range(qb)], axis=1).astype(F32)
        out = []
        for h in range(hpg):
            s = jnp.dot(k_ref[pl.ds(r0, ch), h * HEAD:(h + 1) * HEAD], qts[h],
                        preferred_element_type=F32) + bias
            s_ref[h, pl.ds(r0, ch), :] = s
            out.append(jnp.maximum(ms[h], _fold_rows(s, jnp.maximum)))
        return tuple(out)

    ms = lax.fori_loop(0, n_ch, scores, tuple(jnp.full((8, nq), -jnp.inf, F32) for _ in range(hpg)))
    ms = [jnp.max(m, axis=0, keepdims=True) for m in ms]

    def weighted(c, carry):
        r0 = pl.multiple_of(c * ch, ch)
        out = []
        for h in range(hpg):
            l, acc = carry[h]
            p = jnp.exp(s_ref[h, pl.ds(r0, ch), :] - ms[h])
            l = l + _fold_rows(p, jnp.add)
            acc = acc + jnp.dot(vt_ref[c, h * HEAD:(h + 1) * HEAD, :], p.astype(BF16),
                                preferred_element_type=F32)
            out.append((l, acc))
        return tuple(out)

    init = tuple((jnp.zeros((8, nq), F32), jnp.zeros((HEAD, nq), F32)) for _ in range(hpg))
    res = lax.fori_loop(0, n_ch, weighted, init)
    for h in range(hpg):
        l, acc = res[h]
        out = acc / jnp.sum(l, axis=0, keepdims=True)
        for a in range(qb):
            o_ref[a * LANE:(a + 1) * LANE, h * HEAD:(h + 1) * HEAD] = (
                out[:, a * LANE:(a + 1) * LANE].T.astype(o_ref.dtype))


def _attend_prompt(qa, ka, vat, bias, dm):
    hpg = 2 if dm.heads % 2 == 0 else 1
    qb = QUERY_BLOCKS
    nq = qb * LANE
    ng = dm.heads // hpg
    w = hpg * HEAD
    once = pl.Buffered(1)
    return pl.pallas_call(
        functools.partial(_attn_kernel, hpg=hpg, qb=qb, ch=dm.ch, cb=dm.cb),
        grid=(ng, dm.tq_pad // nq),
        in_specs=[pl.BlockSpec((nq, w), lambda g, i: (i, g)),
                  pl.BlockSpec((dm.t_pad, w), lambda g, i: (0, g), pipeline_mode=once),
                  pl.BlockSpec((dm.t_pad // dm.ch, w, dm.ch), lambda g, i: (0, g, 0), pipeline_mode=once),
                  pl.BlockSpec((qb, dm.t_pad, LANE), lambda g, i: (i, 0, 0))],
        out_specs=pl.BlockSpec((nq, w), lambda g, i: (i, g)),
        out_shape=jax.ShapeDtypeStruct((dm.tq_pad, dm.d_r), BF16),
        scratch_shapes=[pltpu.VMEM((hpg, dm.t_pad, nq), F32)],
        compiler_params=_params("parallel", "arbitrary"),
        name="attend_prompt",
    )(qa, ka, vat, bias)


def _sample_index_kernel(pt_ref, *refs, n_pages, page, ih, ds, n_sel, scale):
    del pt_ref
    cik_refs = refs[:n_pages]
    qi_ref, kin_ref, w_ref, bias_ref, key_ref, kall_ref = refs[n_pages:]
    nq = ds * ih
    n_past = n_pages * page
    q2t = jnp.concatenate([qi_ref[r:r + LANE, :].astype(F32).T for r in range(0, nq, LANE)],
                          axis=1).astype(BF16)
    w = w_ref[...] * scale
    sel = (lax.broadcasted_iota(I32, (nq, LANE), 0) // ih
           == lax.broadcasted_iota(I32, (nq, LANE), 1) % ds).astype(BF16)
    for p in range(n_pages):
        kall_ref[p * page:(p + 1) * page, :] = cik_refs[p][...].astype(BF16)
    kall_ref[n_past:, :] = jnp.concatenate([kin_ref[...], jnp.zeros((page - ds, HEAD), F32)],
                                           axis=0).astype(BF16)
    s = jnp.dot(kall_ref[...], q2t, preferred_element_type=F32)
    keys = _to_key(_dot2_rhs(jnp.maximum(s, 0.0) * w, sel))
    key_ref[:n_past, :] = keys[:n_past]
    visible = (lax.broadcasted_iota(I32, (page, LANE), 0) <= lax.broadcasted_iota(I32, (page, LANE), 1) % ds)
    key_ref[n_past:, :] = jnp.where(visible, keys[n_past:], KEY_NEG_INF)

    thr = _kth_largest_replicated(lambda cand: _count_ge(key_ref[...], cand), n_sel, ds)
    bias_ref[...] = _select_bias(key_ref[...], thr).astype(bias_ref.dtype)


def _dot2_rhs(x, b):
    hi = x.astype(BF16)
    lo = (x - hi.astype(F32)).astype(BF16)
    return jnp.dot(hi, b, preferred_element_type=F32) + jnp.dot(lo, b, preferred_element_type=F32)


def _index_select_sample(page_table, cik, qi, ki, w_s, dm):
    assert dm.page == LANE and dm.heads * dm.ds == LANE and dm.t % dm.ds == 0
    nq = dm.ds * dm.ih
    n_keys = (dm.n_pages + 1) * dm.page
    assert n_keys >= dm.nsel_s
    qi3 = qi.reshape(dm.rp * dm.ih, HEAD)
    r0 = dm.t // dm.ds
    page_spec = lambda p: pl.BlockSpec((None, dm.page, HEAD), lambda b, pt, p=p: (pt[b, p], 0, 0))
    grid_spec = pltpu.PrefetchScalarGridSpec(
        num_scalar_prefetch=1,
        grid=(dm.db,),
        in_specs=[page_spec(p) for p in range(dm.n_pages)] + [
            pl.BlockSpec((nq, HEAD), lambda b, pt: (r0 + b, 0)),
            pl.BlockSpec((dm.ds, HEAD), lambda b, pt: (r0 + b, 0)),
            pl.BlockSpec((None, 1, nq), lambda b, pt: (b, 0, 0))],
        out_specs=pl.BlockSpec((None, n_keys, LANE), lambda b, pt: (b, 0, 0)),
        scratch_shapes=[pltpu.VMEM((n_keys, LANE), I32), pltpu.VMEM((n_keys, HEAD), BF16)],
    )
    return pl.pallas_call(
        functools.partial(_sample_index_kernel, n_pages=dm.n_pages, page=dm.page, ih=dm.ih, ds=dm.ds,
                          n_sel=dm.nsel_s, scale=dm.ih ** -0.5 * HEAD ** -0.5),
        grid_spec=grid_spec,
        out_shape=jax.ShapeDtypeStruct((dm.db, n_keys, LANE), BF16),
        compiler_params=_params("parallel"),
        name="index_select_sample",
    )(page_table, *([cik] * dm.n_pages), qi3, ki, w_s)


def _sample_attn_kernel(pt_ref, *refs, pps, page, nh, ds):
    del pt_ref
    k_refs, v_refs = refs[:pps], refs[pps:2 * pps]
    q_ref, kn_ref, vn_ref, bias_ref, o_ref, qt_ref, onehot_ref, headmask_ref, m_ref, l_ref, acc_ref = refs[2 * pps:]
    b, g = pl.program_id(0), pl.program_id(1)
    rows = page * nh

    @pl.when((b == 0) & (g == 0))
    def _():
        r = lax.broadcasted_iota(I32, (rows, LANE), 0)
        c = lax.broadcasted_iota(I32, (rows, LANE), 1)
        onehot_ref[...] = (r // nh == c).astype(BF16)
        headmask_ref[...] = jnp.where(r % nh == c // ds, 0.0, MASKED)

    @pl.when(g == 0)
    def _():
        q8 = q_ref[...]
        qstack = jnp.concatenate([q8[:, h * HEAD:(h + 1) * HEAD] for h in range(nh)], axis=0)
        qt_ref[...] = qstack.T.astype(BF16)
        m_ref[...] = jnp.full_like(m_ref, -jnp.inf)
        l_ref[...] = jnp.zeros_like(l_ref)
        acc_ref[...] = jnp.zeros_like(acc_ref)

    def attend(k2, v2, onehot, headmask, bias_tile):
        rhs = jnp.concatenate([qt_ref[...], bias_tile], axis=0)
        n = k2.shape[0]
        step = min(n, 512)
        s = jnp.concatenate(
            [jnp.dot(jnp.concatenate([k2[r:r + step].astype(BF16), onehot[r:r + step]], axis=1), rhs,
                     preferred_element_type=F32) + headmask[r:r + step] for r in range(0, n, step)], axis=0)
        m_old = m_ref[...]
        m_new = jnp.maximum(m_old, jnp.max(_fold_rows(s, jnp.maximum), axis=0, keepdims=True))
        alpha = jnp.exp(m_old - m_new)
        p = jnp.exp(s - m_new)
        l_ref[...] = alpha * l_ref[...] + jnp.sum(_fold_rows(p, jnp.add), axis=0, keepdims=True)
        acc_ref[...] = alpha * acc_ref[...] + lax.dot_general(v2.astype(BF16), p.astype(BF16), _TN,
                                                              preferred_element_type=F32)
        m_ref[...] = m_new

    for p in range(pps):
        r0 = pl.multiple_of((g * pps + p) * page, page)
        attend(k_refs[p][...], v_refs[p][...], onehot_ref[...], headmask_ref[...], bias_ref[pl.ds(r0, page), :])

    @pl.when(g == pl.num_programs(1) - 1)
    def _():
        n_past = bias_ref.shape[0] - page
        by_head = lambda ref: jnp.concatenate([ref[:, h * HEAD:(h + 1) * HEAD] for h in range(nh)], axis=0)
        r = lax.broadcasted_iota(I32, (nh * ds, LANE), 0)
        c = lax.broadcasted_iota(I32, (nh * ds, LANE), 1)
        attend(by_head(kn_ref), by_head(vn_ref), (r % ds == c).astype(BF16),
               jnp.where(r // ds == c // ds, 0.0, MASKED), bias_ref[n_past:, :])
        out = (acc_ref[...] / l_ref[...]).T
        for h in range(nh):
            o_ref[:, h * HEAD:(h + 1) * HEAD] = out[h * ds:(h + 1) * ds, :]


def _attend_sample(page_table, ck, cv, qa, k_new, v_new, bias, dm):
    assert dm.page == LANE and dm.heads * dm.ds == LANE
    pps = 4 if dm.n_pages % 4 == 0 else 1
    w = dm.heads * HEAD
    rows = dm.page * dm.heads
    n_keys = (dm.n_pages + 1) * dm.page
    r0 = dm.t // dm.ds
    page_spec = lambda p: pl.BlockSpec((None, rows, HEAD), lambda b, g, pt, p=p: (pt[b, g * pps + p], 0, 0))
    grid_spec = pltpu.PrefetchScalarGridSpec(
        num_scalar_prefetch=1,
        grid=(dm.db, dm.n_pages // pps),
        in_specs=[page_spec(p) for p in range(pps)] * 2 + [
            pl.BlockSpec((dm.ds, w), lambda b, g, pt: (r0 + b, 0)),
            pl.BlockSpec((dm.ds, w), lambda b, g, pt: (b, 0)),
            pl.BlockSpec((dm.ds, w), lambda b, g, pt: (b, 0)),
            pl.BlockSpec((None, n_keys, LANE), lambda b, g, pt: (b, 0, 0))],
        out_specs=pl.BlockSpec((dm.ds, w), lambda b, g, pt: (b, 0)),
        scratch_shapes=[pltpu.VMEM((HEAD, LANE), BF16), pltpu.VMEM((rows, LANE), BF16),
                        pltpu.VMEM((rows, LANE), F32), pltpu.VMEM((1, LANE), F32), pltpu.VMEM((1, LANE), F32),
                        pltpu.VMEM((HEAD, LANE), F32)],
    )
    n_pool = ck.shape[0]
    return pl.pallas_call(
        functools.partial(_sample_attn_kernel, pps=pps, page=dm.page, nh=dm.heads, ds=dm.ds),
        grid_spec=grid_spec,
        out_shape=jax.ShapeDtypeStruct((dm.db * dm.ds, w), F32),
        compiler_params=_params("arbitrary", "arbitrary"),
        name="attend_sample",
    )(page_table, *([ck.reshape(n_pool, rows, HEAD)] * pps), *([cv.reshape(n_pool, rows, HEAD)] * pps),
      qa, k_new, v_new, bias)


def _bf16_padded(w, rows, cols):
    w = w.astype(BF16)
    return jnp.pad(w, ((0, rows - w.shape[0]), (0, cols - w.shape[1])))


def kernel(x_prompt, x_sample, cache_k, cache_v, cache_idx_k, state_hgrn, page_table, meta_tokens, lb_logits,
           norm_mix_g, w_in, q_norm_g, w_uq, w_uq_idx, idx_k_norm_g, idx_k_norm_b, hgrn_norm_g, w_out,
           norm_ffn_g, w_gate, w_up, w_down, final_norm_g):
    assert x_prompt.shape[0] == 1 and w_in.shape[0] == 1, "single prompt sequence, single layer"
    dm = _dims(x_prompt, x_sample, cache_k, page_table)
    d, t, r, rp = dm.d, dm.t, dm.r, dm.rp
    ns = dm.db * dm.ds
    off_cq = 4 * dm.d_r
    off_k = off_cq + dm.q_rank
    off_v = off_k + dm.d_r
    off_ik = off_v + dm.d_r

    h0 = jnp.concatenate([meta_tokens.astype(F32), x_prompt[0], x_sample.reshape(ns, d),
                          jnp.zeros((rp - r, d), F32)], axis=0)
    hn = _rmsnorm(h0, norm_mix_g[0], width=d, col_block=0, out_dtype=BF16)
    z = _matmul(hn, _bf16_padded(w_in[0], d, dm.d_in_p), dm, out_dtype=F32, name="in_proj")

    k_all = z[:r, off_k:off_k + dm.d_r]
    v_all = z[:r, off_v:off_v + dm.d_r]

    o_r_p, s_p = _hgrn_prompt(z, lb_logits, hgrn_norm_g[0], dm)
    o_r_s, s_s = _hgrn_sample(z, lb_logits, hgrn_norm_g[0], state_hgrn[0], dm)

    cqn = _rmsnorm(z, q_norm_g[0], width=dm.q_rank, col_block=off_cq // dm.q_rank, out_dtype=BF16)
    qa = _matmul(cqn, w_uq[0].astype(BF16), dm, out_dtype=F32, scale=HEAD ** -0.5, name="q_up")
    qi = _matmul(cqn, w_uq_idx[0].astype(BF16), dm, out_dtype=BF16, name="q_idx_up")
    ki, ki_b, wt = _idx_prep(z, idx_k_norm_g[0], idx_k_norm_b[0], dm, off_ik)

    ka = z[:dm.t_pad, off_k:off_k + dm.d_r].astype(BF16)
    nc = dm.t_pad // dm.ch
    vat = z[:dm.t_pad, off_v:off_v + dm.d_r].astype(BF16).T.reshape(dm.d_r, nc, dm.ch).transpose(1, 0, 2)
    bias_p = _index_select_prompt(qi, ki_b, wt, dm)
    o_a_p = _attend_prompt(qa, ka, vat, bias_p, dm)

    iw_off = off_ik + HEAD
    w_s = z[t:r, iw_off:iw_off + dm.ih].reshape(dm.db, 1, dm.ds * dm.ih)
    bias_s = _index_select_sample(page_table, cache_idx_k[0], qi, ki, w_s, dm)
    o_a_s = _attend_sample(page_table, cache_k[0], cache_v[0], qa, k_all[t:], v_all[t:], bias_s, dm)

    o_mix = jnp.concatenate([
        jnp.concatenate([o_r_p[:t], o_a_p[:t]], axis=1),
        jnp.concatenate([o_r_s, o_a_s], axis=1).astype(BF16),
        jnp.zeros((rp - r, 2 * dm.d_r), BF16)], axis=0)
    h1 = _matmul(o_mix, w_out[0].astype(BF16), dm, out_dtype=F32, residual=h0, name="out_proj")

    hf = _rmsnorm(h1, norm_ffn_g[0], width=d, col_block=0, out_dtype=BF16)
    act = _ffn_up(hf, w_gate[0].astype(BF16), w_up[0].astype(BF16), dm)
    h2 = _ffn_down(act, w_down[0].astype(BF16), h1, dm)
    y = _rmsnorm(h2, final_norm_g, width=d, col_block=0, out_dtype=F32)

    prompt_heads = lambda a: a[:t].reshape(1, 1, t, dm.heads, HEAD)
    sample_heads = lambda a: a[t:].reshape(1, dm.db, dm.ds, dm.heads, HEAD)
    return (y[N_META:t][None],
            y[t:r].reshape(dm.db, dm.ds, d),
            prompt_heads(k_all), prompt_heads(v_all), ki[:t].reshape(1, 1, t, HEAD), s_p[None, None],
            sample_heads(k_all), sample_heads(v_all), ki[t:r].reshape(1, dm.db, dm.ds, HEAD), s_s[None])
```

```python
import functools
from typing import NamedTuple

import jax
import jax.numpy as jnp
from jax import lax
from jax.experimental import pallas as pl
from jax.experimental.pallas import tpu as pltpu

F32 = jnp.float32
BF16 = jnp.bfloat16
I32 = jnp.int32

EPS = 1e-6
N_META = 16
HEAD = 128
TOPK_MAX = 256
LANE = 128
VMEM_LIMIT_BYTES = 56 * 2**20
QUERY_BLOCKS = 2
MASKED = -1e30
INT_MIN = -2**31
KEY_NEG_INF = -2139095041

_NT = (((1,), (1,)), ((), ()))
_TN = (((0,), (0,)), ((), ()))


class Dims(NamedTuple):
    d: int
    t: int
    t_pad: int
    tq_pad: int
    db: int
    ds: int
    r: int
    rp: int
    tm: int
    heads: int
    ih: int
    d_r: int
    q_rank: int
    d_in: int
    d_ff: int
    n_pages: int
    page: int
    nsel_p: int
    nsel_s: int
    cb: int
    ch: int


def _round_up(x, m):
    return -(-x // m) * m


def _dims(x_prompt, x_sample, cache_k, page_table):
    d = x_prompt.shape[-1]
    seq = x_prompt.shape[1]
    db, ds = x_sample.shape[:2]
    t = seq + N_META
    t_pad = _round_up(t, LANE)
    nb = t_pad // LANE
    cb = 5 if nb % 5 == 0 else 1
    tq_pad = _round_up(t, QUERY_BLOCKS * LANE)
    r = t + db * ds
    rp = max(_round_up(r, LANE), tq_pad)
    n_tiles = max(1, -(-rp // 1200))
    while (rp // 16) % n_tiles:
        n_tiles += 1
    heads = d // 256
    d_r = heads * HEAD
    q_rank = d // 4
    ih = d // 128
    d_in = 6 * d_r + q_rank + HEAD + ih
    d_ff = -(-8 * d // (3 * 256)) * 256
    n_pages, page = page_table.shape[1], cache_k.shape[2]
    past = n_pages * page
    return Dims(d=d, t=t, t_pad=t_pad, tq_pad=tq_pad, db=db, ds=ds, r=r, rp=rp, tm=rp // n_tiles, heads=heads, ih=ih,
                d_r=d_r, q_rank=q_rank, d_in=d_in, d_ff=d_ff,
                n_pages=n_pages, page=page,
                nsel_p=min(TOPK_MAX, seq // 4), nsel_s=min(TOPK_MAX, (past + ds) // 4),
                cb=cb, ch=cb * LANE)


def _tile(n, pref):
    best = LANE
    for c in range(LANE, min(n, pref) + 1, LANE):
        if n % c == 0:
            best = c
    return best


def _params(*sem):
    return pltpu.CompilerParams(dimension_semantics=sem, vmem_limit_bytes=VMEM_LIMIT_BYTES)


def _sigmoid(x):
    return 1.0 / (1.0 + jnp.exp(-x))


def _silu(x):
    return x * _sigmoid(x)


def _rmsnorm_kernel(x_ref, g_ref, o_ref):
    x = x_ref[...]
    ms = jnp.mean(x * x, axis=-1, keepdims=True)
    o_ref[...] = (x * lax.rsqrt(ms + EPS) * g_ref[...]).astype(o_ref.dtype)


def _rmsnorm(x, g, *, width, col_block, out_dtype):
    rows = x.shape[0]
    tr = LANE
    return pl.pallas_call(
        _rmsnorm_kernel,
        grid=(rows // tr,),
        in_specs=[pl.BlockSpec((tr, width), lambda i: (i, col_block)),
                  pl.BlockSpec((1, width), lambda i: (0, 0))],
        out_specs=pl.BlockSpec((tr, width), lambda i: (i, 0)),
        out_shape=jax.ShapeDtypeStruct((rows, width), out_dtype),
        compiler_params=_params("parallel"),
        name="rmsnorm",
    )(x, g.reshape(1, width).astype(F32))


def _idx_prep_kernel(ik_ref, iw_ref, g_ref, b_ref, ki_ref, kib_ref, wt_ref, *, ih, scale):
    x = ik_ref[...]
    mu = jnp.mean(x, axis=-1, keepdims=True)
    xc = x - mu
    var = jnp.mean(xc * xc, axis=-1, keepdims=True)
    ki = xc * lax.rsqrt(var + EPS) * g_ref[...] + b_ref[...]
    ki_ref[...] = ki
    kib_ref[...] = ki.astype(BF16)
    wt_ref[...] = (iw_ref[...] * scale).T[:ih, :]


def _idx_prep(z, g, b, dm, off_ik):
    tr = LANE
    cb = off_ik // HEAD
    return pl.pallas_call(
        functools.partial(_idx_prep_kernel, ih=dm.ih, scale=dm.ih ** -0.5 * HEAD ** -0.5),
        grid=(dm.rp // tr,),
        in_specs=[pl.BlockSpec((tr, HEAD), lambda i: (i, cb)),
                  pl.BlockSpec((tr, HEAD), lambda i: (i, cb + 1)),
                  pl.BlockSpec((1, HEAD), lambda i: (0, 0)),
                  pl.BlockSpec((1, HEAD), lambda i: (0, 0))],
        out_specs=[pl.BlockSpec((tr, HEAD), lambda i: (i, 0)),
                   pl.BlockSpec((tr, HEAD), lambda i: (i, 0)),
                   pl.BlockSpec((dm.ih, tr), lambda i: (0, i))],
        out_shape=[jax.ShapeDtypeStruct((dm.rp, HEAD), F32),
                   jax.ShapeDtypeStruct((dm.rp, HEAD), BF16),
                   jax.ShapeDtypeStruct((dm.ih, dm.rp), F32)],
        compiler_params=_params("parallel"),
        name="idx_prep",
    )(z, z, g.reshape(1, HEAD), b.reshape(1, HEAD))


def _heads_out_kernel(x0_ref, x1_ref, o_ref, *, nh):
    half = nh // 2
    for h in range(nh):
        src = x0_ref if h < half else x1_ref
        o_ref[:, h, :] = src[:, (h % half) * HEAD:(h % half + 1) * HEAD]


def _prompt_heads(z, off, dm):
    wblk = dm.d_r // 2
    tr = max(c for c in range(8, 513, 8) if dm.t % c == 0)
    cb = off // wblk
    assert off % wblk == 0 and dm.heads % 2 == 0
    return pl.pallas_call(
        functools.partial(_heads_out_kernel, nh=dm.heads),
        grid=(dm.t // tr,),
        in_specs=[pl.BlockSpec((tr, wblk), lambda i: (i, cb)),
                  pl.BlockSpec((tr, wblk), lambda i: (i, cb + 1))],
        out_specs=pl.BlockSpec((tr, dm.heads, HEAD), lambda i: (i, 0, 0)),
        out_shape=jax.ShapeDtypeStruct((dm.t, dm.heads, HEAD), F32),
        compiler_params=_params("parallel"),
        name="prompt_heads",
    )(z, z)


def _attn_operands_kernel(k0_ref, k1_ref, v0_ref, v1_ref, ka_ref, vat_ref):
    w = k0_ref.shape[1]
    ka_ref[:, :w] = k0_ref[...].astype(BF16)
    ka_ref[:, w:] = k1_ref[...].astype(BF16)
    for c0 in range(0, w, LANE):
        vat_ref[c0:c0 + LANE, :] = v0_ref[:, c0:c0 + LANE].T.astype(BF16)
        vat_ref[w + c0:w + c0 + LANE, :] = v1_ref[:, c0:c0 + LANE].T.astype(BF16)


def _attn_operands(z, off_k, off_v, dm):
    wblk = dm.d_r // 2
    ck, cv = off_k // wblk, off_v // wblk
    assert off_k % wblk == 0 and off_v % wblk == 0
    blk = lambda c: pl.BlockSpec((LANE, wblk), lambda i, c=c: (i, c))
    return pl.pallas_call(
        _attn_operands_kernel,
        grid=(dm.t_pad // LANE,),
        in_specs=[blk(ck), blk(ck + 1), blk(cv), blk(cv + 1)],
        out_specs=[pl.BlockSpec((LANE, dm.d_r), lambda i: (i, 0)),
                   pl.BlockSpec((None, dm.d_r, LANE), lambda i: (i // dm.cb, 0, i % dm.cb))],
        out_shape=[jax.ShapeDtypeStruct((dm.t_pad, dm.d_r), BF16),
                   jax.ShapeDtypeStruct((dm.t_pad // dm.ch, dm.d_r, dm.ch), BF16)],
        compiler_params=_params("parallel"),
        name="attn_operands",
    )(z, z, z, z)


def _weight_tile(w_ref):
    w = w_ref[...]
    return w if w.dtype == BF16 else w.astype(BF16)


def _mm_kernel(a_ref, b_ref, o_ref, *, scale):
    acc = jnp.dot(a_ref[...], _weight_tile(b_ref), preferred_element_type=F32)
    if scale != 1.0:
        acc = acc * scale
    o_ref[...] = acc.astype(o_ref.dtype)


def _mm_res_kernel(a_ref, b_ref, r_ref, o_ref):
    o_ref[...] = r_ref[...] + jnp.dot(a_ref[...], _weight_tile(b_ref), preferred_element_type=F32)


def _matmul(a, b, dm, *, out_dtype, name, scale=1.0, residual=None, n_cols=None, tn_max=512):
    m, k = a.shape
    n = b.shape[1] if n_cols is None else n_cols
    tm, tn = dm.tm, _tile(n, tn_max)
    a_mode = {"pipeline_mode": pl.Buffered(1)} if tm * k * a.dtype.itemsize > 16 * 2**20 else {}
    in_specs = [pl.BlockSpec((tm, k), lambda i, j: (i, 0), **a_mode),
                pl.BlockSpec((k, tn), lambda i, j: (0, j))]
    args = [a, b]
    if residual is None:
        body = functools.partial(_mm_kernel, scale=scale)
    else:
        body = _mm_res_kernel
        in_specs.append(pl.BlockSpec((tm, tn), lambda i, j: (i, j)))
        args.append(residual)
    return pl.pallas_call(
        body,
        grid=(m // tm, n // tn),
        in_specs=in_specs,
        out_specs=pl.BlockSpec((tm, tn), lambda i, j: (i, j)),
        out_shape=jax.ShapeDtypeStruct((m, n), out_dtype),
        compiler_params=_params("parallel", "parallel"),
        name=name,
    )(*args)


def _ffn_up_kernel(x_ref, wg_ref, wu_ref, o_ref):
    x = x_ref[...]
    g = jnp.dot(x, _weight_tile(wg_ref), preferred_element_type=F32)
    u = jnp.dot(x, _weight_tile(wu_ref), preferred_element_type=F32)
    o_ref[...] = (_silu(g) * u).astype(o_ref.dtype)


def _ffn_up(x, wg, wu, dm):
    m, k = x.shape
    n = wg.shape[1]
    tm, tn = dm.tm, _tile(n, 256)
    return pl.pallas_call(
        _ffn_up_kernel,
        grid=(m // tm, n // tn),
        in_specs=[pl.BlockSpec((tm, k), lambda i, j: (i, 0)),
                  pl.BlockSpec((k, tn), lambda i, j: (0, j)),
                  pl.BlockSpec((k, tn), lambda i, j: (0, j))],
        out_specs=pl.BlockSpec((tm, tn), lambda i, j: (i, j)),
        out_shape=jax.ShapeDtypeStruct((m, n), BF16),
        compiler_params=_params("parallel", "parallel"),
        name="ffn_up",
    )(x, wg, wu)


def _split3(x):
    hi = x.astype(BF16)
    r = x - hi.astype(F32)
    mid = r.astype(BF16)
    lo = (r - mid.astype(F32)).astype(BF16)
    return hi, mid, lo


def _dot3(a, x):
    hi, mid, lo = _split3(x)
    return (jnp.dot(a, hi, preferred_element_type=F32) + jnp.dot(a, mid, preferred_element_type=F32)
            + jnp.dot(a, lo, preferred_element_type=F32))


def _lower_bound(lbl):
    e = jnp.exp(lbl - jnp.max(lbl, axis=0, keepdims=True))
    return e[0:1, :] / jnp.sum(e, axis=0, keepdims=True)


def _hgrn_gates(q_r, f_r, lb):
    key = (1.0 - lb) * _sigmoid(-f_r)
    return _silu(q_r), key, jnp.log(1.0 - key)


def _diag_block(q, key, g, v, ones):
    n = q.shape[0]
    rows = lax.broadcasted_iota(I32, (n, HEAD), 0)
    parts = []
    for s in range(n):
        decay = jnp.exp(jnp.minimum(g - g[s:s + 1, :], 0.0))
        parts.append(jnp.where(rows >= s, q * (key[s:s + 1, :] * decay), 0.0))
    p = jnp.concatenate(parts, axis=0).astype(BF16)
    a = jnp.dot(p, ones, preferred_element_type=F32)
    o = a[0:n] * v[0:1, :]
    for s in range(1, n):
        o = o + a[s * n:(s + 1) * n] * v[s:s + 1, :]
    return o


def _hgrn_levels(q, key, g):
    c = q.shape[0]
    row = lax.broadcasted_iota(I32, (c, HEAD), 0)
    t_i = lax.broadcasted_iota(I32, (c, c), 0)
    s_i = lax.broadcasted_iota(I32, (c, c), 1)
    total = None
    b = c // 2
    while b >= 16:
        gref = g[b - 1:b, :]
        for m in range(2 * b, c, 2 * b):
            gref = jnp.where(row >= m, g[m + b - 1:m + b, :], gref)
        lower = (row & b) != 0
        qs = jnp.where(lower, q * jnp.exp(jnp.minimum(g - gref, 0.0)), 0.0)
        ks = jnp.where(lower, 0.0, key * jnp.exp(jnp.minimum(gref - g, 0.0)))
        a = lax.dot_general(qs.astype(BF16), ks.astype(BF16), _NT, preferred_element_type=F32)
        if 2 * b < c:
            a = jnp.where((t_i // (2 * b)) == (s_i // (2 * b)), a, 0.0)
        total = a if total is None else total + a
        b //= 2
    return total


def _scan_rows8(x):
    row = lax.broadcasted_iota(I32, x.shape, 0)
    for sh in (1, 2, 4):
        x = x + jnp.where(row >= sh, pltpu.roll(x, sh, 0), 0.0)
    return x


def _hgrn_out(o, g_norm, og):
    ms = jnp.mean(o * o, axis=-1, keepdims=True)
    return o * lax.rsqrt(ms + EPS) * g_norm * _silu(og)


def _hgrn_prompt_kernel(q_ref, f_ref, i_ref, og_ref, lbl_ref, gn_ref, o_ref, s_ref, st_ref, *, t):
    c = pl.program_id(1)

    @pl.when(c == 0)
    def _():
        st_ref[...] = jnp.zeros_like(st_ref)

    q, key, lf = _hgrn_gates(q_ref[...], f_ref[...], _lower_bound(lbl_ref[...]))
    v = i_ref[...]
    valid = (c * LANE + lax.broadcasted_iota(I32, (LANE, HEAD), 0)) < t
    key = jnp.where(valid, key, 0.0)
    lf = jnp.where(valid, lf, 0.0)

    tri = (lax.broadcasted_iota(I32, (LANE, LANE), 0) >= lax.broadcasted_iota(I32, (LANE, LANE), 1))
    g = _dot3(tri.astype(BF16), lf)
    g_last = g[LANE - 1:LANE, :]
    st = st_ref[...]
    vb = v.astype(BF16)
    o = lax.dot_general((q * jnp.exp(g)).astype(BF16), st.astype(BF16), _NT, preferred_element_type=F32)
    o = o + jnp.dot(_hgrn_levels(q, key, g).astype(BF16), vb, preferred_element_type=F32)
    ones = jnp.ones((HEAD, HEAD), BF16)
    o = o + jnp.concatenate(
        [_diag_block(q[m:m + 16], key[m:m + 16], g[m:m + 16], v[m:m + 16], ones) for m in range(0, LANE, 16)],
        axis=0)
    kd = (key * jnp.exp(g_last - g)).astype(BF16)
    st_new = jnp.exp(g_last) * st + lax.dot_general(vb, kd, _TN, preferred_element_type=F32)
    st_ref[...] = st_new
    o_ref[...] = _hgrn_out(o, gn_ref[...], og_ref[...]).astype(o_ref.dtype)

    @pl.when(c == pl.num_programs(1) - 1)
    def _():
        s_ref[...] = st_new.T


def _hgrn_prompt(z, lb_logits, g_norm, dm):
    nh, nc = dm.heads, dm.t_pad // LANE
    blk = lambda off: pl.BlockSpec((LANE, HEAD), lambda h, c, off=off: (c, off * nh + h))
    nl = lb_logits.shape[0]
    return pl.pallas_call(
        functools.partial(_hgrn_prompt_kernel, t=dm.t),
        grid=(nh, nc),
        in_specs=[blk(0), blk(1), blk(2), blk(3),
                  pl.BlockSpec((nl, HEAD), lambda h, c: (0, h)),
                  pl.BlockSpec((1, HEAD), lambda h, c: (0, 0))],
        out_specs=[pl.BlockSpec((LANE, HEAD), lambda h, c: (c, h)),
                   pl.BlockSpec((None, HEAD, HEAD), lambda h, c: (h, 0, 0))],
        out_shape=[jax.ShapeDtypeStruct((dm.t_pad, dm.d_r), BF16),
                   jax.ShapeDtypeStruct((nh, HEAD, HEAD), F32)],
        scratch_shapes=[pltpu.VMEM((HEAD, HEAD), F32)],
        compiler_params=_params("parallel", "arbitrary"),
        name="hgrn_prompt",
    )(z, z, z, z, lb_logits, g_norm.reshape(1, HEAD))


def _hgrn_sample_kernel(q_ref, f_ref, i_ref, og_ref, lbl_ref, gn_ref, s0_ref, o_ref, s_ref, *, nh, ds):
    q, key, lf = _hgrn_gates(q_ref[...], f_ref[...], _lower_bound(lbl_ref[...]))
    v = i_ref[...]
    g = _scan_rows8(lf)
    g_last = g[ds - 1:ds, :]
    qe = q * jnp.exp(g)
    kd = key * jnp.exp(g_last - g)
    e_last = jnp.exp(g_last)
    ones = jnp.ones((HEAD, HEAD), BF16)
    zpad = jnp.zeros((LANE - ds, HEAD), F32)
    og = og_ref[...]
    gn = gn_ref[...]
    for h in range(nh):
        sl = slice(h * HEAD, (h + 1) * HEAD)
        st = s0_ref[h].T
        o = lax.dot_general(qe[:, sl].astype(BF16), st.astype(BF16), _NT, preferred_element_type=F32)
        o = o + _diag_block(q[:, sl], key[:, sl], g[:, sl], v[:, sl], ones)
        vp = jnp.concatenate([v[:, sl], zpad], axis=0).astype(BF16)
        kp = jnp.concatenate([kd[:, sl], zpad], axis=0).astype(BF16)
        st_new = e_last[:, sl] * st + lax.dot_general(vp, kp, _TN, preferred_element_type=F32)
        s_ref[h] = st_new.T
        o_ref[:, sl] = _hgrn_out(o, gn, og[:, sl])


def _hgrn_sample(z, lb_logits, g_norm, state, dm):
    assert dm.ds == 8 and dm.t % dm.ds == 0
    nh, r0 = dm.heads, dm.t // dm.ds
    blk = lambda off: pl.BlockSpec((dm.ds, dm.d_r), lambda b, off=off: (r0 + b, off))
    nl = lb_logits.shape[0]
    return pl.pallas_call(
        functools.partial(_hgrn_sample_kernel, nh=nh, ds=dm.ds),
        grid=(dm.db,),
        in_specs=[blk(0), blk(1), blk(2), blk(3),
                  pl.BlockSpec((nl, dm.d_r), lambda b: (0, 0)),
                  pl.BlockSpec((1, HEAD), lambda b: (0, 0)),
                  pl.BlockSpec((None, nh, HEAD, HEAD), lambda b: (b, 0, 0, 0))],
        out_specs=[pl.BlockSpec((dm.ds, dm.d_r), lambda b: (b, 0)),
                   pl.BlockSpec((None, nh, HEAD, HEAD), lambda b: (b, 0, 0, 0))],
        out_shape=[jax.ShapeDtypeStruct((dm.db * dm.ds, dm.d_r), F32),
                   jax.ShapeDtypeStruct((dm.db, nh, HEAD, HEAD), F32)],
        compiler_params=_params("parallel"),
        name="hgrn_sample",
    )(z, z, z, z, lb_logits, g_norm.reshape(1, HEAD), state)


def _to_key(x):
    b = lax.bitcast_convert_type(x, I32)
    return b ^ (lax.shift_right_arithmetic(b, 31) & 0x7FFFFFFF)


def _fold_rows(x, op):
    total = None
    for r0 in range(0, x.shape[0], 256):
        parts = [x[r:r + 8] for r in range(r0, min(r0 + 256, x.shape[0]), 8)]
        while len(parts) > 1:
            parts = [op(parts[j], parts[j + 1]) for j in range(0, len(parts) - 1, 2)] + parts[len(parts) & ~1:]
        total = parts[0] if total is None else op(total, parts[0])
    return total


def _count_ge(keys, cand):
    return _fold_rows((keys >= cand).astype(I32), jnp.add)


def _kth_largest(count_ge, n_sel):
    def body(it, tp):
        bit = jnp.left_shift(jnp.int32(1), 31 - it)
        cand = tp | bit
        cnt = jnp.sum(count_ge(cand ^ INT_MIN), axis=0, keepdims=True)
        return jnp.where(cnt >= n_sel, cand, tp)
    return lax.fori_loop(0, 32, body, jnp.zeros((1, LANE), I32)) ^ INT_MIN


def _kth_largest_replicated(count_ge, n_sel, period):
    copies = LANE // period
    bits = copies.bit_length() - 1
    assert copies == 1 << bits and 32 % bits == 0
    digit = lax.broadcasted_iota(I32, (8, LANE), 1) // period

    def body(it, tp):
        shift = 32 - bits * (it + 1)
        cand = tp | jnp.left_shift(digit, shift)
        cnt = jnp.sum(count_ge(cand[0:1] ^ INT_MIN), axis=0, keepdims=True)
        ok = jnp.broadcast_to((cnt >= n_sel).astype(I32), (8, LANE))
        step = period
        while step < LANE:
            ok = ok + pltpu.roll(ok, step, 1)
            step *= 2
        return tp | jnp.left_shift(ok - 1, shift)

    tp = lax.fori_loop(0, 32 // bits, body, jnp.zeros((8, LANE), I32))
    return tp[0:1] ^ INT_MIN


def _select_bias(keys, thr):
    return jnp.where((keys >= thr) & (keys > KEY_NEG_INF), 0.0, MASKED)


def _index_kernel(qi_ref, ki_ref, wt_ref, bias_ref, key_ref, qt_ref, *, ih, ch, cb, n_sel):
    i = pl.program_id(0)
    n_tiles = jnp.minimum(i + 1, key_ref.shape[0] // LANE)
    n_ch = (n_tiles + cb - 1) // cb
    n_total = key_ref.shape[0] // ch
    for hp in range(ih // 2):
        pair = [qi_ref[:, (2 * hp + e) * HEAD:(2 * hp + e + 1) * HEAD].astype(F32).T for e in (0, 1)]
        qt_ref[hp] = jnp.concatenate(pair, axis=1).astype(BF16)
    q_pos = i * LANE + lax.broadcasted_iota(I32, (LANE, LANE), 1)
    k_off = lax.broadcasted_iota(I32, (LANE, LANE), 0)

    def score_tile(c, carry):
        r0 = pl.multiple_of(c * LANE, LANE)
        kc = ki_ref[pl.ds(r0, LANE), :]
        acc = jnp.zeros((LANE, LANE), F32)
        for hp in range(ih // 2):
            s = jnp.dot(kc, qt_ref[hp], preferred_element_type=F32)
            acc = acc + jnp.maximum(s[:, :LANE], 0.0) * wt_ref[pl.ds(2 * hp, 1), :]
            acc = acc + jnp.maximum(s[:, LANE:], 0.0) * wt_ref[pl.ds(2 * hp + 1, 1), :]
        key_ref[pl.ds(r0, LANE), :] = jnp.where(r0 + k_off <= q_pos, _to_key(acc), KEY_NEG_INF)
        return carry

    lax.fori_loop(0, n_tiles, score_tile, 0)

    def fill_tile(c, carry):
        key_ref[pl.ds(pl.multiple_of(c * LANE, LANE), LANE), :] = jnp.full((LANE, LANE), KEY_NEG_INF, I32)
        return carry

    lax.fori_loop(n_tiles, n_ch * cb, fill_tile, 0)

    def count_ge(cand):
        def body(c, cnt):
            return cnt + _count_ge(key_ref[pl.ds(pl.multiple_of(c * ch, ch), ch), :], cand)
        return lax.fori_loop(0, n_ch, body, jnp.zeros((8, LANE), I32))

    thr = _kth_largest(count_ge, n_sel)

    def write_bias(c, carry):
        r0 = pl.multiple_of(c * ch, ch)
        bias_ref[pl.ds(r0, ch), :] = _select_bias(key_ref[pl.ds(r0, ch), :], thr).astype(bias_ref.dtype)
        return carry

    lax.fori_loop(0, n_ch, write_bias, 0)

    def write_masked(c, carry):
        bias_ref[pl.ds(pl.multiple_of(c * ch, ch), ch), :] = jnp.full((ch, LANE), MASKED, bias_ref.dtype)
        return carry

    lax.fori_loop(n_ch, n_total, write_masked, 0)


def _index_select_prompt(qi, ki_b, wt, dm):
    nb = dm.tq_pad // LANE
    assert dm.ch >= dm.nsel_p and dm.ih % 2 == 0
    return pl.pallas_call(
        functools.partial(_index_kernel, ih=dm.ih, ch=dm.ch, cb=dm.cb, n_sel=dm.nsel_p),
        grid=(nb,),
        in_specs=[pl.BlockSpec((LANE, dm.ih * HEAD), lambda i: (i, 0)),
                  pl.BlockSpec((dm.t_pad, HEAD), lambda i: (0, 0)),
                  pl.BlockSpec((dm.ih, LANE), lambda i: (0, i))],
        out_specs=pl.BlockSpec((None, dm.t_pad, LANE), lambda i: (i, 0, 0)),
        out_shape=jax.ShapeDtypeStruct((nb, dm.t_pad, LANE), BF16),
        scratch_shapes=[pltpu.VMEM((dm.t_pad, LANE), I32), pltpu.VMEM((dm.ih // 2, HEAD, 2 * LANE), BF16)],
        compiler_params=_params("parallel"),
        name="index_select_prompt",
    )(qi, ki_b, wt)


def _attn_kernel(q_ref, k_ref, vt_ref, bias_ref, o_ref, s_ref, *, hpg, qb, ch, cb):
    i = pl.program_id(1)
    n_tiles = jnp.minimum(qb * (i + 1), k_ref.shape[0] // LANE)
    n_ch = (n_tiles + cb - 1) // cb
    nq = qb * LANE
    qts = [jnp.concatenate([q_ref[a * LANE:(a + 1) * LANE, h * HEAD:(h + 1) * HEAD].T for a in range(qb)],
                           axis=1).astype(BF16) for h in range(hpg)]

    def scores(c, ms):
        r0 = pl.multiple_of(c * ch, ch)
        bias = jnp.concatenate([bias_ref[a, pl.ds(r0, ch), :] for a in range(qb)], axis=1).astype(F32)
        out = []
        for h in range(hpg):
            s = jnp.dot(k_ref[pl.ds(r0, ch), h * HEAD:(h + 1) * HEAD], qts[h],
                        preferred_element_type=F32) + bias
            s_ref[h, pl.ds(r0, ch), :] = s
            out.append(jnp.maximum(ms[h], _fold_rows(s, jnp.maximum)))
        return tuple(out)

    ms = lax.fori_loop(0, n_ch, scores, tuple(jnp.full((8, nq), -jnp.inf, F32) for _ in range(hpg)))
    ms = [jnp.max(m, axis=0, keepdims=True) for m in ms]

    def weighted(c, carry):
        r0 = pl.multiple_of(c * ch, ch)
        out = []
        for h in range(hpg):
            l, acc = carry[h]
            p = jnp.exp(s_ref[h, pl.ds(r0, ch), :] - ms[h])
            l = l + _fold_rows(p, jnp.add)
            acc = acc + jnp.dot(vt_ref[c, h * HEAD:(h + 1) * HEAD, :], p.astype(BF16),
                                preferred_element_type=F32)
            out.append((l, acc))
        return tuple(out)

    init = tuple((jnp.zeros((8, nq), F32), jnp.zeros((HEAD, nq), F32)) for _ in range(hpg))
    res = lax.fori_loop(0, n_ch, weighted, init)
    for h in range(hpg):
        l, acc = res[h]
        out = acc / jnp.sum(l, axis=0, keepdims=True)
        for a in range(qb):
            o_ref[a * LANE:(a + 1) * LANE, h * HEAD:(h + 1) * HEAD] = (
                out[:, a * LANE:(a + 1) * LANE].T.astype(o_ref.dtype))


def _attend_prompt(qa, ka, vat, bias, dm):
    hpg = 2 if dm.heads % 2 == 0 else 1
    qb = QUERY_BLOCKS
    nq = qb * LANE
    ng = dm.heads // hpg
    w = hpg * HEAD
    once = pl.Buffered(1)
    return pl.pallas_call(
        functools.partial(_attn_kernel, hpg=hpg, qb=qb, ch=dm.ch, cb=dm.cb),
        grid=(ng, dm.tq_pad // nq),
        in_specs=[pl.BlockSpec((nq, w), lambda g, i: (i, g)),
                  pl.BlockSpec((dm.t_pad, w), lambda g, i: (0, g), pipeline_mode=once),
                  pl.BlockSpec((dm.t_pad // dm.ch, w, dm.ch), lambda g, i: (0, g, 0), pipeline_mode=once),
                  pl.BlockSpec((qb, dm.t_pad, LANE), lambda g, i: (i, 0, 0))],
        out_specs=pl.BlockSpec((nq, w), lambda g, i: (i, g)),
        out_shape=jax.ShapeDtypeStruct((dm.tq_pad, dm.d_r), BF16),
        scratch_shapes=[pltpu.VMEM((hpg, dm.t_pad, nq), F32)],
        compiler_params=_params("parallel", "arbitrary"),
        name="attend_prompt",
    )(qa, ka, vat, bias)


def _sample_index_kernel(pt_ref, *refs, n_pages, page, ih, ds, n_sel, scale):
    del pt_ref
    cik_refs = refs[:n_pages]
    qi_ref, kin_ref, w_ref, bias_ref, key_ref, kall_ref = refs[n_pages:]
    nq = ds * ih
    n_past = n_pages * page
    q2t = jnp.concatenate([qi_ref[r:r + LANE, :].astype(F32).T for r in range(0, nq, LANE)],
                          axis=1).astype(BF16)
    w = w_ref[...] * scale
    sel = (lax.broadcasted_iota(I32, (nq, LANE), 0) // ih
           == lax.broadcasted_iota(I32, (nq, LANE), 1) % ds).astype(BF16)
    for p in range(n_pages):
        kall_ref[p * page:(p + 1) * page, :] = cik_refs[p][...].astype(BF16)
    kall_ref[n_past:, :] = jnp.concatenate([kin_ref[...], jnp.zeros((page - ds, HEAD), F32)],
                                           axis=0).astype(BF16)
    s = jnp.dot(kall_ref[...], q2t, preferred_element_type=F32)
    keys = _to_key(_dot2_rhs(jnp.maximum(s, 0.0) * w, sel))
    key_ref[:n_past, :] = keys[:n_past]
    visible = (lax.broadcasted_iota(I32, (page, LANE), 0) <= lax.broadcasted_iota(I32, (page, LANE), 1) % ds)
    key_ref[n_past:, :] = jnp.where(visible, keys[n_past:], KEY_NEG_INF)

    thr = _kth_largest_replicated(lambda cand: _count_ge(key_ref[...], cand), n_sel, ds)
    bias_ref[...] = _select_bias(key_ref[...], thr).astype(bias_ref.dtype)


def _dot2_rhs(x, b):
    hi = x.astype(BF16)
    lo = (x - hi.astype(F32)).astype(BF16)
    return jnp.dot(hi, b, preferred_element_type=F32) + jnp.dot(lo, b, preferred_element_type=F32)


def _index_select_sample(page_table, cik, qi, ki, w_s, dm):
    assert dm.page == LANE and dm.heads * dm.ds == LANE and dm.t % dm.ds == 0
    nq = dm.ds * dm.ih
    n_keys = (dm.n_pages + 1) * dm.page
    assert n_keys >= dm.nsel_s
    qi3 = qi.reshape(dm.rp * dm.ih, HEAD)
    r0 = dm.t // dm.ds
    page_spec = lambda p: pl.BlockSpec((None, dm.page, HEAD), lambda b, pt, p=p: (pt[b, p], 0, 0))
    grid_spec = pltpu.PrefetchScalarGridSpec(
        num_scalar_prefetch=1,
        grid=(dm.db,),
        in_specs=[page_spec(p) for p in range(dm.n_pages)] + [
            pl.BlockSpec((nq, HEAD), lambda b, pt: (r0 + b, 0)),
            pl.BlockSpec((dm.ds, HEAD), lambda b, pt: (r0 + b, 0)),
            pl.BlockSpec((None, 1, nq), lambda b, pt: (b, 0, 0))],
        out_specs=pl.BlockSpec((None, n_keys, LANE), lambda b, pt: (b, 0, 0)),
        scratch_shapes=[pltpu.VMEM((n_keys, LANE), I32), pltpu.VMEM((n_keys, HEAD), BF16)],
    )
    return pl.pallas_call(
        functools.partial(_sample_index_kernel, n_pages=dm.n_pages, page=dm.page, ih=dm.ih, ds=dm.ds,
                          n_sel=dm.nsel_s, scale=dm.ih ** -0.5 * HEAD ** -0.5),
        grid_spec=grid_spec,
        out_shape=jax.ShapeDtypeStruct((dm.db, n_keys, LANE), BF16),
        compiler_params=_params("parallel"),
        name="index_select_sample",
    )(page_table, *([cik] * dm.n_pages), qi3, ki, w_s)


def _sample_attn_kernel(pt_ref, *refs, pps, page, nh, ds):
    del pt_ref
    k_refs, v_refs = refs[:pps], refs[pps:2 * pps]
    q_ref, kn_ref, vn_ref, bias_ref, o_ref, qt_ref, onehot_ref, headmask_ref, m_ref, l_ref, acc_ref = refs[2 * pps:]
    b, g = pl.program_id(0), pl.program_id(1)
    rows = page * nh

    @pl.when((b == 0) & (g == 0))
    def _():
        r = lax.broadcasted_iota(I32, (rows, LANE), 0)
        c = lax.broadcasted_iota(I32, (rows, LANE), 1)
        onehot_ref[...] = (r // nh == c).astype(BF16)
        headmask_ref[...] = jnp.where(r % nh == c // ds, 0.0, MASKED)

    @pl.when(g == 0)
    def _():
        q8 = q_ref[...]
        qstack = jnp.concatenate([q8[:, h * HEAD:(h + 1) * HEAD] for h in range(nh)], axis=0)
        qt_ref[...] = qstack.T.astype(BF16)
        m_ref[...] = jnp.full_like(m_ref, -jnp.inf)
        l_ref[...] = jnp.zeros_like(l_ref)
        acc_ref[...] = jnp.zeros_like(acc_ref)

    def attend(k2, v2, onehot, headmask, bias_tile):
        rhs = jnp.concatenate([qt_ref[...], bias_tile], axis=0)
        n = k2.shape[0]
        step = min(n, 512)
        s = jnp.concatenate(
            [jnp.dot(jnp.concatenate([k2[r:r + step].astype(BF16), onehot[r:r + step]], axis=1), rhs,
                     preferred_element_type=F32) + headmask[r:r + step] for r in range(0, n, step)], axis=0)
        m_old = m_ref[...]
        m_new = jnp.maximum(m_old, jnp.max(_fold_rows(s, jnp.maximum), axis=0, keepdims=True))
        alpha = jnp.exp(m_old - m_new)
        p = jnp.exp(s - m_new)
        l_ref[...] = alpha * l_ref[...] + jnp.sum(_fold_rows(p, jnp.add), axis=0, keepdims=True)
        acc_ref[...] = alpha * acc_ref[...] + lax.dot_general(v2.astype(BF16), p.astype(BF16), _TN,
                                                              preferred_element_type=F32)
        m_ref[...] = m_new

    for p in range(pps):
        r0 = pl.multiple_of((g * pps + p) * page, page)
        attend(k_refs[p][...], v_refs[p][...], onehot_ref[...], headmask_ref[...], bias_ref[pl.ds(r0, page), :])

    @pl.when(g == pl.num_programs(1) - 1)
    def _():
        n_past = bias_ref.shape[0] - page
        by_head = lambda ref: jnp.concatenate([ref[:, h * HEAD:(h + 1) * HEAD] for h in range(nh)], axis=0)
        r = lax.broadcasted_iota(I32, (nh * ds, LANE), 0)
        c = lax.broadcasted_iota(I32, (nh * ds, LANE), 1)
        attend(by_head(kn_ref), by_head(vn_ref), (r % ds == c).astype(BF16),
               jnp.where(r // ds == c // ds, 0.0, MASKED), bias_ref[n_past:, :])
        out = (acc_ref[...] / l_ref[...]).T
        for h in range(nh):
            o_ref[:, h * HEAD:(h + 1) * HEAD] = out[h * ds:(h + 1) * ds, :]


def _attend_sample(page_table, ck, cv, qa, k_new, v_new, bias, dm):
    assert dm.page == LANE and dm.heads * dm.ds == LANE
    pps = 4 if dm.n_pages % 4 == 0 else 1
    w = dm.heads * HEAD
    rows = dm.page * dm.heads
    n_keys = (dm.n_pages + 1) * dm.page
    r0 = dm.t // dm.ds
    page_spec = lambda p: pl.BlockSpec((None, rows, HEAD), lambda b, g, pt, p=p: (pt[b, g * pps + p], 0, 0))
    grid_spec = pltpu.PrefetchScalarGridSpec(
        num_scalar_prefetch=1,
        grid=(dm.db, dm.n_pages // pps),
        in_specs=[page_spec(p) for p in range(pps)] * 2 + [
            pl.BlockSpec((dm.ds, w), lambda b, g, pt: (r0 + b, 0)),
            pl.BlockSpec((dm.ds, w), lambda b, g, pt: (b, 0)),
            pl.BlockSpec((dm.ds, w), lambda b, g, pt: (b, 0)),
            pl.BlockSpec((None, n_keys, LANE), lambda b, g, pt: (b, 0, 0))],
        out_specs=pl.BlockSpec((dm.ds, w), lambda b, g, pt: (b, 0)),
        scratch_shapes=[pltpu.VMEM((HEAD, LANE), BF16), pltpu.VMEM((rows, LANE), BF16),
                        pltpu.VMEM((rows, LANE), F32), pltpu.VMEM((1, LANE), F32), pltpu.VMEM((1, LANE), F32),
                        pltpu.VMEM((HEAD, LANE), F32)],
    )
    n_pool = ck.shape[0]
    return pl.pallas_call(
        functools.partial(_sample_attn_kernel, pps=pps, page=dm.page, nh=dm.heads, ds=dm.ds),
        grid_spec=grid_spec,
        out_shape=jax.ShapeDtypeStruct((dm.db * dm.ds, w), F32),
        compiler_params=_params("arbitrary", "arbitrary"),
        name="attend_sample",
    )(page_table, *([ck.reshape(n_pool, rows, HEAD)] * pps), *([cv.reshape(n_pool, rows, HEAD)] * pps),
      qa, k_new, v_new, bias)


def kernel(x_prompt, x_sample, cache_k, cache_v, cache_idx_k, state_hgrn, page_table, meta_tokens, lb_logits,
           norm_mix_g, w_in, q_norm_g, w_uq, w_uq_idx, idx_k_norm_g, idx_k_norm_b, hgrn_norm_g, w_out,
           norm_ffn_g, w_gate, w_up, w_down, final_norm_g):
    assert x_prompt.shape[0] == 1 and w_in.shape[0] == 1, "single prompt sequence, single layer"
    dm = _dims(x_prompt, x_sample, cache_k, page_table)
    d, t, r, rp = dm.d, dm.t, dm.r, dm.rp
    ns = dm.db * dm.ds
    off_cq = 4 * dm.d_r
    off_k = off_cq + dm.q_rank
    off_v = off_k + dm.d_r
    off_ik = off_v + dm.d_r

    h0 = jnp.concatenate([meta_tokens.astype(F32), x_prompt[0], x_sample.reshape(ns, d),
                          jnp.zeros((rp - r, d), F32)], axis=0)
    hn = _rmsnorm(h0, norm_mix_g[0], width=d, col_block=0, out_dtype=BF16)
    z = _matmul(hn, w_in[0], dm, out_dtype=F32, n_cols=off_ik, name="in_proj")
    w_tail = jnp.pad(w_in[0][:, off_ik:], ((0, 0), (0, 2 * HEAD - (dm.d_in - off_ik))))
    z_tail = _matmul(hn, w_tail, dm, out_dtype=F32, name="in_proj_idx")

    k_s = z[t:r, off_k:off_k + dm.d_r]
    v_s = z[t:r, off_v:off_v + dm.d_r]

    o_r_p, s_p = _hgrn_prompt(z, lb_logits, hgrn_norm_g[0], dm)
    o_r_s, s_s = _hgrn_sample(z, lb_logits, hgrn_norm_g[0], state_hgrn[0], dm)

    cqn = _rmsnorm(z, q_norm_g[0], width=dm.q_rank, col_block=off_cq // dm.q_rank, out_dtype=BF16)
    qa = _matmul(cqn, w_uq[0], dm, out_dtype=F32, scale=HEAD ** -0.5, name="q_up")
    qi = _matmul(cqn, w_uq_idx[0], dm, out_dtype=BF16, name="q_idx_up")
    ki, ki_b, wt = _idx_prep(z_tail, idx_k_norm_g[0], idx_k_norm_b[0], dm, 0)

    ka, vat = _attn_operands(z, off_k, off_v, dm)
    bias_p = _index_select_prompt(qi, ki_b, wt, dm)
    o_a_p = _attend_prompt(qa, ka, vat, bias_p, dm)

    w_s = z_tail[t:r, HEAD:HEAD + dm.ih].reshape(dm.db, 1, dm.ds * dm.ih)
    bias_s = _index_select_sample(page_table, cache_idx_k[0], qi, ki, w_s, dm)
    o_a_s = _attend_sample(page_table, cache_k[0], cache_v[0], qa, k_s, v_s, bias_s, dm)

    o_mix = jnp.concatenate([
        jnp.concatenate([o_r_p[:t], o_a_p[:t]], axis=1),
        jnp.concatenate([o_r_s, o_a_s], axis=1).astype(BF16),
        jnp.zeros((rp - r, 2 * dm.d_r), BF16)], axis=0)
    h1 = _matmul(o_mix, w_out[0], dm, out_dtype=F32, residual=h0, name="out_proj")

    hf = _rmsnorm(h1, norm_ffn_g[0], width=d, col_block=0, out_dtype=BF16)
    act = _ffn_up(hf, w_gate[0], w_up[0], dm)
    h2 = _matmul(act, w_down[0].astype(BF16), dm, out_dtype=F32, residual=h1, tn_max=256, name="ffn_down")
    y = _rmsnorm(h2, final_norm_g, width=d, col_block=0, out_dtype=F32)

    sample_heads = lambda a: a.reshape(1, dm.db, dm.ds, dm.heads, HEAD)
    return (y[N_META:t][None],
            y[t:r].reshape(dm.db, dm.ds, d),
            _prompt_heads(z, off_k, dm)[None, None], _prompt_heads(z, off_v, dm)[None, None],
            ki[:t].reshape(1, 1, t, HEAD), s_p[None, None],
            sample_heads(k_s), sample_heads(v_s), ki[t:r].reshape(1, dm.db, dm.ds, HEAD), s_s[None])
```

```python
import functools
from typing import NamedTuple

import jax
import jax.numpy as jnp
from jax import lax
from jax.experimental import pallas as pl
from jax.experimental.pallas import tpu as pltpu

F32 = jnp.float32
BF16 = jnp.bfloat16
I32 = jnp.int32

EPS = 1e-6
N_META = 16
HEAD = 128
TOPK_MAX = 256
LANE = 128
VMEM_LIMIT_BYTES = 56 * 2**20
QUERY_BLOCKS = 2
MASKED = -1e30
INT_MIN = -2**31
KEY_NEG_INF = -2139095041

_NT = (((1,), (1,)), ((), ()))
_TN = (((0,), (0,)), ((), ()))


class Dims(NamedTuple):
    d: int
    t: int
    t_pad: int
    tq_pad: int
    tk_pad: int
    db: int
    ds: int
    r: int
    rp: int
    tm: int
    heads: int
    ih: int
    d_r: int
    q_rank: int
    d_in: int
    d_ff: int
    n_pages: int
    page: int
    nsel_p: int
    nsel_s: int
    cb: int
    ch: int


def _round_up(x, m):
    return -(-x // m) * m


def _dims(x_prompt, x_sample, cache_k, page_table):
    d = x_prompt.shape[-1]
    seq = x_prompt.shape[1]
    db, ds = x_sample.shape[:2]
    t = seq + N_META
    t_pad = _round_up(t, LANE)
    nb = t_pad // LANE
    cb = 5 if nb % 5 == 0 else 1
    tq_pad = _round_up(t, QUERY_BLOCKS * LANE)
    tk_pad = _round_up(t_pad, 2 * cb * LANE)
    r = t + db * ds
    rp = max(_round_up(r, LANE), tq_pad, tk_pad)
    n_tiles = max(1, -(-rp // 1200))
    while (rp // 16) % n_tiles:
        n_tiles += 1
    heads = d // 256
    d_r = heads * HEAD
    q_rank = d // 4
    ih = d // 128
    d_in = 6 * d_r + q_rank + HEAD + ih
    d_ff = -(-8 * d // (3 * 256)) * 256
    n_pages, page = page_table.shape[1], cache_k.shape[2]
    past = n_pages * page
    return Dims(d=d, t=t, t_pad=t_pad, tq_pad=tq_pad, tk_pad=tk_pad, db=db, ds=ds, r=r, rp=rp, tm=rp // n_tiles, heads=heads, ih=ih,
                d_r=d_r, q_rank=q_rank, d_in=d_in, d_ff=d_ff,
                n_pages=n_pages, page=page,
                nsel_p=min(TOPK_MAX, seq // 4), nsel_s=min(TOPK_MAX, (past + ds) // 4),
                cb=cb, ch=cb * LANE)


def _tile(n, pref):
    best = LANE
    for c in range(LANE, min(n, pref) + 1, LANE):
        if n % c == 0:
            best = c
    return best


def _params(*sem):
    return pltpu.CompilerParams(dimension_semantics=sem, vmem_limit_bytes=VMEM_LIMIT_BYTES)


def _sigmoid(x):
    return 1.0 / (1.0 + jnp.exp(-x))


def _silu(x):
    return x * _sigmoid(x)


def _rmsnorm_kernel(x_ref, g_ref, o_ref):
    x = x_ref[...]
    ms = jnp.mean(x * x, axis=-1, keepdims=True)
    o_ref[...] = (x * lax.rsqrt(ms + EPS) * g_ref[...]).astype(o_ref.dtype)


def _rmsnorm(x, g, *, width, col_block, out_dtype):
    rows = x.shape[0]
    tr = LANE
    return pl.pallas_call(
        _rmsnorm_kernel,
        grid=(rows // tr,),
        in_specs=[pl.BlockSpec((tr, width), lambda i: (i, col_block)),
                  pl.BlockSpec((1, width), lambda i: (0, 0))],
        out_specs=pl.BlockSpec((tr, width), lambda i: (i, 0)),
        out_shape=jax.ShapeDtypeStruct((rows, width), out_dtype),
        compiler_params=_params("parallel"),
        name="rmsnorm",
    )(x, g.reshape(1, width).astype(F32))


def _rmsnorm_rows(x, g, row0, n_rows):
    width = x.shape[1]
    tr = max(c for c in range(8, LANE + 1, 8) if n_rows % c == 0)
    assert row0 % 8 == 0
    return pl.pallas_call(
        _rmsnorm_kernel,
        grid=(n_rows // tr,),
        in_specs=[pl.BlockSpec((pl.Element(tr), pl.Element(width)), lambda i: (pl.multiple_of(row0 + i * tr, 8), 0)),
                  pl.BlockSpec((1, width), lambda i: (0, 0))],
        out_specs=pl.BlockSpec((tr, width), lambda i: (i, 0)),
        out_shape=jax.ShapeDtypeStruct((n_rows, width), F32),
        compiler_params=_params("parallel"),
        name="rmsnorm_rows",
    )(x, g.reshape(1, width).astype(F32))


def _idx_prep_kernel(ik_ref, iw_ref, g_ref, b_ref, ki_ref, kib_ref, wt_ref, *, ih, scale):
    x = ik_ref[...]
    mu = jnp.mean(x, axis=-1, keepdims=True)
    xc = x - mu
    var = jnp.mean(xc * xc, axis=-1, keepdims=True)
    ki = xc * lax.rsqrt(var + EPS) * g_ref[...] + b_ref[...]
    ki_ref[...] = ki
    kib_ref[...] = ki.astype(BF16)
    wt_ref[...] = (iw_ref[...] * scale).T[:ih, :]


def _idx_prep(z, g, b, dm, off_ik):
    tr = LANE
    cb = off_ik // HEAD
    return pl.pallas_call(
        functools.partial(_idx_prep_kernel, ih=dm.ih, scale=dm.ih ** -0.5 * HEAD ** -0.5),
        grid=(dm.rp // tr,),
        in_specs=[pl.BlockSpec((tr, HEAD), lambda i: (i, cb)),
                  pl.BlockSpec((tr, HEAD), lambda i: (i, cb + 1)),
                  pl.BlockSpec((1, HEAD), lambda i: (0, 0)),
                  pl.BlockSpec((1, HEAD), lambda i: (0, 0))],
        out_specs=[pl.BlockSpec((tr, HEAD), lambda i: (i, 0)),
                   pl.BlockSpec((tr, HEAD), lambda i: (i, 0)),
                   pl.BlockSpec((dm.ih, tr), lambda i: (0, i))],
        out_shape=[jax.ShapeDtypeStruct((dm.rp, HEAD), F32),
                   jax.ShapeDtypeStruct((dm.rp, HEAD), BF16),
                   jax.ShapeDtypeStruct((dm.ih, dm.rp), F32)],
        compiler_params=_params("parallel"),
        name="idx_prep",
    )(z, z, g.reshape(1, HEAD), b.reshape(1, HEAD))


def _heads_out_kernel(x0_ref, x1_ref, o_ref, *, nh):
    half = nh // 2
    for h in range(nh):
        src = x0_ref if h < half else x1_ref
        o_ref[:, h, :] = src[:, (h % half) * HEAD:(h % half + 1) * HEAD]


def _prompt_heads(z, off, dm):
    wblk = dm.d_r // 2
    tr = max(c for c in range(8, 513, 8) if dm.t % c == 0)
    cb = off // wblk
    assert off % wblk == 0 and dm.heads % 2 == 0
    return pl.pallas_call(
        functools.partial(_heads_out_kernel, nh=dm.heads),
        grid=(dm.t // tr,),
        in_specs=[pl.BlockSpec((tr, wblk), lambda i: (i, cb)),
                  pl.BlockSpec((tr, wblk), lambda i: (i, cb + 1))],
        out_specs=pl.BlockSpec((tr, dm.heads, HEAD), lambda i: (i, 0, 0)),
        out_shape=jax.ShapeDtypeStruct((dm.t, dm.heads, HEAD), F32),
        compiler_params=_params("parallel"),
        name="prompt_heads",
    )(z, z)


def _attn_operands_kernel(k0_ref, k1_ref, v0_ref, v1_ref, ka_ref, vat_ref):
    w = k0_ref.shape[1]
    ka_ref[:, :w] = k0_ref[...].astype(BF16)
    ka_ref[:, w:] = k1_ref[...].astype(BF16)
    for c0 in range(0, w, LANE):
        vat_ref[c0:c0 + LANE, :] = v0_ref[:, c0:c0 + LANE].T.astype(BF16)
        vat_ref[w + c0:w + c0 + LANE, :] = v1_ref[:, c0:c0 + LANE].T.astype(BF16)


def _attn_operands(z, off_k, off_v, dm):
    wblk = dm.d_r // 2
    ck, cv = off_k // wblk, off_v // wblk
    assert off_k % wblk == 0 and off_v % wblk == 0
    blk = lambda c: pl.BlockSpec((LANE, wblk), lambda i, c=c: (i, c))
    return pl.pallas_call(
        _attn_operands_kernel,
        grid=(dm.tk_pad // LANE,),
        in_specs=[blk(ck), blk(ck + 1), blk(cv), blk(cv + 1)],
        out_specs=[pl.BlockSpec((LANE, dm.d_r), lambda i: (i, 0)),
                   pl.BlockSpec((None, dm.d_r, LANE), lambda i: (i // dm.cb, 0, i % dm.cb))],
        out_shape=[jax.ShapeDtypeStruct((dm.tk_pad, dm.d_r), BF16),
                   jax.ShapeDtypeStruct((dm.tk_pad // dm.ch, dm.d_r, dm.ch), BF16)],
        compiler_params=_params("parallel"),
        name="attn_operands",
    )(z, z, z, z)


def _weight_tile(w_ref):
    w = w_ref[...]
    return w if w.dtype == BF16 else w.astype(BF16)


def _mm_kernel(a_ref, b_ref, o_ref, *, scale):
    acc = jnp.dot(a_ref[...], _weight_tile(b_ref), preferred_element_type=F32)
    if scale != 1.0:
        acc = acc * scale
    o_ref[...] = acc.astype(o_ref.dtype)


def _mm_res_kernel(a_ref, b_ref, r_ref, o_ref):
    o_ref[...] = r_ref[...] + jnp.dot(a_ref[...], _weight_tile(b_ref), preferred_element_type=F32)


def _matmul(a, b, dm, *, out_dtype, name, scale=1.0, residual=None, n_cols=None, tn_max=512):
    m, k = a.shape
    n = b.shape[1] if n_cols is None else n_cols
    tm, tn = dm.tm, _tile(n, tn_max)
    a_mode = {"pipeline_mode": pl.Buffered(1)} if tm * k * a.dtype.itemsize > 16 * 2**20 else {}
    in_specs = [pl.BlockSpec((tm, k), lambda i, j: (i, 0), **a_mode),
                pl.BlockSpec((k, tn), lambda i, j: (0, j))]
    args = [a, b]
    if residual is None:
        body = functools.partial(_mm_kernel, scale=scale)
    else:
        body = _mm_res_kernel
        in_specs.append(pl.BlockSpec((tm, tn), lambda i, j: (i, j)))
        args.append(residual)
    return pl.pallas_call(
        body,
        grid=(m // tm, n // tn),
        in_specs=in_specs,
        out_specs=pl.BlockSpec((tm, tn), lambda i, j: (i, j)),
        out_shape=jax.ShapeDtypeStruct((m, n), out_dtype),
        compiler_params=_params("parallel", "parallel"),
        name=name,
    )(*args)


def _ffn_up_kernel(x_ref, wg_ref, wu_ref, o_ref):
    x = x_ref[...]
    g = jnp.dot(x, _weight_tile(wg_ref), preferred_element_type=F32)
    u = jnp.dot(x, _weight_tile(wu_ref), preferred_element_type=F32)
    o_ref[...] = (_silu(g) * u).astype(o_ref.dtype)


def _ffn_up(x, wg, wu, dm):
    m, k = x.shape
    n = wg.shape[1]
    tm, tn = dm.tm, _tile(n, 256)
    return pl.pallas_call(
        _ffn_up_kernel,
        grid=(m // tm, n // tn),
        in_specs=[pl.BlockSpec((tm, k), lambda i, j: (i, 0)),
                  pl.BlockSpec((k, tn), lambda i, j: (0, j)),
                  pl.BlockSpec((k, tn), lambda i, j: (0, j))],
        out_specs=pl.BlockSpec((tm, tn), lambda i, j: (i, j)),
        out_shape=jax.ShapeDtypeStruct((m, n), BF16),
        compiler_params=_params("parallel", "parallel"),
        name="ffn_up",
    )(x, wg, wu)


def _split3(x):
    hi = x.astype(BF16)
    r = x - hi.astype(F32)
    mid = r.astype(BF16)
    lo = (r - mid.astype(F32)).astype(BF16)
    return hi, mid, lo


def _dot3(a, x):
    hi, mid, lo = _split3(x)
    return (jnp.dot(a, hi, preferred_element_type=F32) + jnp.dot(a, mid, preferred_element_type=F32)
            + jnp.dot(a, lo, preferred_element_type=F32))


def _lower_bound(lbl):
    e = jnp.exp(lbl - jnp.max(lbl, axis=0, keepdims=True))
    return e[0:1, :] / jnp.sum(e, axis=0, keepdims=True)


def _hgrn_gates(q_r, f_r, lb):
    key = (1.0 - lb) * _sigmoid(-f_r)
    return _silu(q_r), key, jnp.log(1.0 - key)


def _diag_block(q, key, g, v, ones):
    n = q.shape[0]
    rows = lax.broadcasted_iota(I32, (n, HEAD), 0)
    parts = []
    for s in range(n):
        lo = s // 8 * 8
        decay = jnp.exp(jnp.minimum(g[lo:] - g[s:s + 1, :], 0.0))
        live = jnp.where(rows[lo:] >= s, q[lo:] * (key[s:s + 1, :] * decay), 0.0)
        parts.append(jnp.concatenate([jnp.zeros((lo, HEAD), F32), live], axis=0) if lo else live)
    p = jnp.concatenate(parts, axis=0).astype(BF16)
    a = jnp.dot(p, ones, preferred_element_type=F32)
    o_groups = [None] * (n // 8)
    for s in range(n):
        for gi in range(s // 8, n // 8):
            term = a[s * n + gi * 8:s * n + gi * 8 + 8] * v[s:s + 1, :]
            o_groups[gi] = term if o_groups[gi] is None else o_groups[gi] + term
    return jnp.concatenate(o_groups, axis=0) if len(o_groups) > 1 else o_groups[0]


def _hgrn_levels(q, key, g):
    c = q.shape[0]
    row = lax.broadcasted_iota(I32, (c, HEAD), 0)
    t_i = lax.broadcasted_iota(I32, (c, c), 0)
    s_i = lax.broadcasted_iota(I32, (c, c), 1)
    total = None
    b = c // 2
    while b >= 16:
        gref = g[b - 1:b, :]
        for m in range(2 * b, c, 2 * b):
            gref = jnp.where(row >= m, g[m + b - 1:m + b, :], gref)
        lower = (row & b) != 0
        qs = jnp.where(lower, q * jnp.exp(jnp.minimum(g - gref, 0.0)), 0.0)
        ks = jnp.where(lower, 0.0, key * jnp.exp(jnp.minimum(gref - g, 0.0)))
        a = lax.dot_general(qs.astype(BF16), ks.astype(BF16), _NT, preferred_element_type=F32)
        if 2 * b < c:
            a = jnp.where((t_i // (2 * b)) == (s_i // (2 * b)), a, 0.0)
        total = a if total is None else total + a
        b //= 2
    return total


def _scan_rows8(x):
    row = lax.broadcasted_iota(I32, x.shape, 0)
    for sh in (1, 2, 4):
        x = x + jnp.where(row >= sh, pltpu.roll(x, sh, 0), 0.0)
    return x


def _hgrn_out(o, g_norm, og):
    ms = jnp.mean(o * o, axis=-1, keepdims=True)
    return o * lax.rsqrt(ms + EPS) * g_norm * _silu(og)


def _hgrn_prompt_kernel(q_ref, f_ref, i_ref, og_ref, lbl_ref, gn_ref, o_ref, s_ref, st_ref, *, t, hpg):
    c = pl.program_id(1)

    @pl.when(c == 0)
    def _():
        st_ref[...] = jnp.zeros_like(st_ref)

    valid = (c * LANE + lax.broadcasted_iota(I32, (LANE, HEAD), 0)) < t
    tri = (lax.broadcasted_iota(I32, (LANE, LANE), 0) >= lax.broadcasted_iota(I32, (LANE, LANE), 1)).astype(BF16)
    ones = jnp.ones((HEAD, HEAD), BF16)
    for h in range(hpg):
        sl = slice(h * HEAD, (h + 1) * HEAD)
        q, key, lf = _hgrn_gates(q_ref[:, sl], f_ref[:, sl], _lower_bound(lbl_ref[:, sl]))
        v = i_ref[:, sl]
        key = jnp.where(valid, key, 0.0)
        lf = jnp.where(valid, lf, 0.0)
        g = _dot3(tri, lf)
        g_last = g[LANE - 1:LANE, :]
        st = st_ref[h]
        vb = v.astype(BF16)
        o = lax.dot_general((q * jnp.exp(g)).astype(BF16), st.astype(BF16), _NT, preferred_element_type=F32)
        o = o + jnp.dot(_hgrn_levels(q, key, g).astype(BF16), vb, preferred_element_type=F32)
        o = o + jnp.concatenate(
            [_diag_block(q[m:m + 16], key[m:m + 16], g[m:m + 16], v[m:m + 16], ones) for m in range(0, LANE, 16)],
            axis=0)
        kd = (key * jnp.exp(g_last - g)).astype(BF16)
        st_new = jnp.exp(g_last) * st + lax.dot_general(vb, kd, _TN, preferred_element_type=F32)
        st_ref[h] = st_new
        o_ref[:, sl] = _hgrn_out(o, gn_ref[...], og_ref[:, sl]).astype(o_ref.dtype)

    @pl.when(c == pl.num_programs(1) - 1)
    def _():
        for h in range(hpg):
            s_ref[h] = st_ref[h].T


def _hgrn_prompt(z, lb_logits, g_norm, dm):
    hpg = 2 if dm.heads % 2 == 0 else 1
    ng, nc, w = dm.heads // hpg, dm.t_pad // LANE, hpg * HEAD
    blk = lambda off: pl.BlockSpec((LANE, w), lambda g, c, off=off: (c, off * ng + g))
    nl = lb_logits.shape[0]
    return pl.pallas_call(
        functools.partial(_hgrn_prompt_kernel, t=dm.t, hpg=hpg),
        grid=(ng, nc),
        in_specs=[blk(0), blk(1), blk(2), blk(3),
                  pl.BlockSpec((nl, w), lambda g, c: (0, g)),
                  pl.BlockSpec((1, HEAD), lambda g, c: (0, 0))],
        out_specs=[pl.BlockSpec((LANE, w), lambda g, c: (c, g)),
                   pl.BlockSpec((hpg, HEAD, HEAD), lambda g, c: (g, 0, 0))],
        out_shape=[jax.ShapeDtypeStruct((dm.t_pad, dm.d_r), BF16),
                   jax.ShapeDtypeStruct((dm.heads, HEAD, HEAD), F32)],
        scratch_shapes=[pltpu.VMEM((hpg, HEAD, HEAD), F32)],
        compiler_params=_params("parallel", "arbitrary"),
        name="hgrn_prompt",
    )(z, z, z, z, lb_logits, g_norm.reshape(1, HEAD))


def _hgrn_sample_kernel(q_ref, f_ref, i_ref, og_ref, lbl_ref, gn_ref, s0_ref, o_ref, s_ref, *, nh, ds):
    q, key, lf = _hgrn_gates(q_ref[...], f_ref[...], _lower_bound(lbl_ref[...]))
    v = i_ref[...]
    g = _scan_rows8(lf)
    g_last = g[ds - 1:ds, :]
    qe = q * jnp.exp(g)
    kd = key * jnp.exp(g_last - g)
    e_last = jnp.exp(g_last)
    ones = jnp.ones((HEAD, HEAD), BF16)
    zpad = jnp.zeros((LANE - ds, HEAD), F32)
    og = og_ref[...]
    gn = gn_ref[...]
    for h in range(nh):
        sl = slice(h * HEAD, (h + 1) * HEAD)
        st = s0_ref[h].T
        o = lax.dot_general(qe[:, sl].astype(BF16), st.astype(BF16), _NT, preferred_element_type=F32)
        o = o + _diag_block(q[:, sl], key[:, sl], g[:, sl], v[:, sl], ones)
        vp = jnp.concatenate([v[:, sl], zpad], axis=0).astype(BF16)
        kp = jnp.concatenate([kd[:, sl], zpad], axis=0).astype(BF16)
        st_new = e_last[:, sl] * st + lax.dot_general(vp, kp, _TN, preferred_element_type=F32)
        s_ref[h] = st_new.T
        o_ref[:, sl] = _hgrn_out(o, gn, og[:, sl])


def _hgrn_sample(z, lb_logits, g_norm, state, dm):
    assert dm.ds == 8 and dm.t % dm.ds == 0
    nh, r0 = dm.heads, dm.t // dm.ds
    blk = lambda off: pl.BlockSpec((dm.ds, dm.d_r), lambda b, off=off: (r0 + b, off))
    nl = lb_logits.shape[0]
    return pl.pallas_call(
        functools.partial(_hgrn_sample_kernel, nh=nh, ds=dm.ds),
        grid=(dm.db,),
        in_specs=[blk(0), blk(1), blk(2), blk(3),
                  pl.BlockSpec((nl, dm.d_r), lambda b: (0, 0)),
                  pl.BlockSpec((1, HEAD), lambda b: (0, 0)),
                  pl.BlockSpec((None, nh, HEAD, HEAD), lambda b: (b, 0, 0, 0))],
        out_specs=[pl.BlockSpec((dm.ds, dm.d_r), lambda b: (b, 0)),
                   pl.BlockSpec((None, nh, HEAD, HEAD), lambda b: (b, 0, 0, 0))],
        out_shape=[jax.ShapeDtypeStruct((dm.db * dm.ds, dm.d_r), F32),
                   jax.ShapeDtypeStruct((dm.db, nh, HEAD, HEAD), F32)],
        compiler_params=_params("parallel"),
        name="hgrn_sample",
    )(z, z, z, z, lb_logits, g_norm.reshape(1, HEAD), state)


def _to_key(x):
    b = lax.bitcast_convert_type(x, I32)
    return b ^ (lax.shift_right_arithmetic(b, 31) & 0x7FFFFFFF)


def _fold_rows(x, op):
    total = None
    for r0 in range(0, x.shape[0], 256):
        parts = [x[r:r + 8] for r in range(r0, min(r0 + 256, x.shape[0]), 8)]
        while len(parts) > 1:
            parts = [op(parts[j], parts[j + 1]) for j in range(0, len(parts) - 1, 2)] + parts[len(parts) & ~1:]
        total = parts[0] if total is None else op(total, parts[0])
    return total


def _count_ge(keys, cand):
    return _fold_rows((keys >= cand).astype(I32), jnp.add)


def _kth_largest(count_ge, n_sel):
    def body(it, tp):
        bit = jnp.left_shift(jnp.int32(1), 31 - it)
        cand = tp | bit
        cnt = jnp.sum(count_ge(cand ^ INT_MIN), axis=0, keepdims=True)
        return jnp.where(cnt >= n_sel, cand, tp)
    return lax.fori_loop(0, 32, body, jnp.zeros((1, LANE), I32)) ^ INT_MIN


def _kth_largest_replicated(count_ge, n_sel, period):
    copies = LANE // period
    bits = copies.bit_length() - 1
    assert copies == 1 << bits and 32 % bits == 0
    digit = lax.broadcasted_iota(I32, (8, LANE), 1) // period

    def body(it, tp):
        shift = 32 - bits * (it + 1)
        cand = tp | jnp.left_shift(digit, shift)
        cnt = jnp.sum(count_ge(cand[0:1] ^ INT_MIN), axis=0, keepdims=True)
        ok = jnp.broadcast_to((cnt >= n_sel).astype(I32), (8, LANE))
        step = period
        while step < LANE:
            ok = ok + pltpu.roll(ok, step, 1)
            step *= 2
        return tp | jnp.left_shift(ok - 1, shift)

    tp = lax.fori_loop(0, 32 // bits, body, jnp.zeros((8, LANE), I32))
    return tp[0:1] ^ INT_MIN


def _select_bias(keys, thr):
    return jnp.where((keys >= thr) & (keys > KEY_NEG_INF), 0.0, MASKED)


def _index_kernel(qi_ref, ki_ref, wt_ref, bias_ref, key_ref, qt_ref, *, ih, ch, cb, n_sel):
    i = pl.program_id(0)
    n_tiles = jnp.minimum(i + 1, key_ref.shape[0] // LANE)
    n_ch = (n_tiles + cb - 1) // cb
    n_total = bias_ref.shape[0] // ch
    for hp in range(ih // 2):
        pair = [qi_ref[:, (2 * hp + e) * HEAD:(2 * hp + e + 1) * HEAD].T for e in (0, 1)]
        qt_ref[hp] = jnp.concatenate(pair, axis=1).astype(BF16)
    q_pos = i * LANE + lax.broadcasted_iota(I32, (LANE, LANE), 1)
    k_off = lax.broadcasted_iota(I32, (LANE, LANE), 0)

    def score_tile(c, carry):
        r0 = pl.multiple_of(c * LANE, LANE)
        kc = ki_ref[pl.ds(r0, LANE), :]
        acc = jnp.zeros((LANE, LANE), F32)
        for hp in range(ih // 2):
            s = jnp.dot(kc, qt_ref[hp], preferred_element_type=F32)
            acc = acc + jnp.maximum(s[:, :LANE], 0.0) * wt_ref[pl.ds(2 * hp, 1), :]
            acc = acc + jnp.maximum(s[:, LANE:], 0.0) * wt_ref[pl.ds(2 * hp + 1, 1), :]
        key_ref[pl.ds(r0, LANE), :] = jnp.where(r0 + k_off <= q_pos, _to_key(acc), KEY_NEG_INF)
        return carry

    lax.fori_loop(0, n_tiles, score_tile, 0)

    def fill_tile(c, carry):
        key_ref[pl.ds(pl.multiple_of(c * LANE, LANE), LANE), :] = jnp.full((LANE, LANE), KEY_NEG_INF, I32)
        return carry

    lax.fori_loop(n_tiles, n_ch * cb, fill_tile, 0)

    def count_ge(cand):
        def body(c, cnt):
            return cnt + _count_ge(key_ref[pl.ds(pl.multiple_of(c * ch, ch), ch), :], cand)
        return lax.fori_loop(0, n_ch, body, jnp.zeros((8, LANE), I32))

    thr = _kth_largest(count_ge, n_sel)

    def write_bias(c, carry):
        r0 = pl.multiple_of(c * ch, ch)
        bias_ref[pl.ds(r0, ch), :] = _select_bias(key_ref[pl.ds(r0, ch), :], thr).astype(bias_ref.dtype)
        return carry

    lax.fori_loop(0, n_ch, write_bias, 0)

    def write_masked(c, carry):
        bias_ref[pl.ds(pl.multiple_of(c * ch, ch), ch), :] = jnp.full((ch, LANE), MASKED, bias_ref.dtype)
        return carry

    lax.fori_loop(n_ch, n_total, write_masked, 0)


def _index_select_prompt(qi, ki_b, wt, dm):
    nb = dm.tq_pad // LANE
    assert dm.ch >= dm.nsel_p and dm.ih % 2 == 0
    return pl.pallas_call(
        functools.partial(_index_kernel, ih=dm.ih, ch=dm.ch, cb=dm.cb, n_sel=dm.nsel_p),
        grid=(nb,),
        in_specs=[pl.BlockSpec((LANE, dm.ih * HEAD), lambda i: (i, 0)),
                  pl.BlockSpec((dm.t_pad, HEAD), lambda i: (0, 0)),
                  pl.BlockSpec((dm.ih, LANE), lambda i: (0, i))],
        out_specs=pl.BlockSpec((None, dm.tk_pad, LANE), lambda i: (i, 0, 0)),
        out_shape=jax.ShapeDtypeStruct((nb, dm.tk_pad, LANE), BF16),
        scratch_shapes=[pltpu.VMEM((dm.t_pad, LANE), I32), pltpu.VMEM((dm.ih // 2, HEAD, 2 * LANE), BF16)],
        compiler_params=_params("parallel"),
        name="index_select_prompt",
    )(qi, ki_b, wt)


def _attn_kernel(q_ref, k_ref, vt_ref, bias_ref, o_ref, sa_ref, sb_ref, *, hpg, qb, ch, cb, t_tiles):
    i = pl.program_id(1)
    n_ch = (jnp.minimum(qb * (i + 1), t_tiles) + cb - 1) // cb
    n_total = k_ref.shape[0] // ch
    assert n_total % 2 == 0
    nq = qb * LANE
    qts = [jnp.concatenate([q_ref[a * LANE:(a + 1) * LANE, h * HEAD:(h + 1) * HEAD].T for a in range(qb)],
                           axis=1).astype(BF16) for h in range(hpg)]

    def put_scores(c, s_ref):
        r0 = pl.multiple_of(c * ch, ch)
        bias = jnp.concatenate([bias_ref[a, pl.ds(r0, ch), :] for a in range(qb)], axis=1).astype(F32)
        for h in range(hpg):
            s_ref[h] = jnp.dot(k_ref[pl.ds(r0, ch), h * HEAD:(h + 1) * HEAD], qts[h],
                               preferred_element_type=F32) + bias

    def consume(c, s_ref, carry):
        m_new = [jnp.maximum(carry[h][0], jnp.max(_fold_rows(s_ref[h], jnp.maximum), axis=0, keepdims=True))
                 for h in range(hpg)]
        out = []
        for h in range(hpg):
            m, l, acc = carry[h]
            alpha = jnp.exp(m - m_new[h])
            p = jnp.exp(s_ref[h] - m_new[h])
            l = alpha * l + _fold_rows(p, jnp.add)
            acc = alpha * acc + jnp.dot(vt_ref[c, h * HEAD:(h + 1) * HEAD, :], p.astype(BF16),
                                        preferred_element_type=F32)
            out.append((m_new[h], l, acc))
        return tuple(out)

    put_scores(0, sa_ref)

    def pair(j, carry):
        put_scores(2 * j + 1, sb_ref)
        carry = consume(2 * j, sa_ref, carry)
        put_scores(jnp.minimum(2 * j + 2, n_total - 1), sa_ref)
        return consume(2 * j + 1, sb_ref, carry)

    init = tuple((jnp.full((1, nq), -jnp.inf, F32), jnp.zeros((8, nq), F32), jnp.zeros((HEAD, nq), F32))
                 for _ in range(hpg))
    res = lax.fori_loop(0, (n_ch + 1) // 2, pair, init)
    for h in range(hpg):
        _, l, acc = res[h]
        out = acc / jnp.sum(l, axis=0, keepdims=True)
        for a in range(qb):
            o_ref[a * LANE:(a + 1) * LANE, h * HEAD:(h + 1) * HEAD] = (
                out[:, a * LANE:(a + 1) * LANE].T.astype(o_ref.dtype))


def _attend_prompt(qa, ka, vat, bias, dm):
    hpg = 4 if dm.heads % 4 == 0 else 1
    qb = QUERY_BLOCKS
    nq = qb * LANE
    ng = dm.heads // hpg
    w = hpg * HEAD
    once = pl.Buffered(1)
    return pl.pallas_call(
        functools.partial(_attn_kernel, hpg=hpg, qb=qb, ch=dm.ch, cb=dm.cb, t_tiles=dm.t_pad // LANE),
        grid=(ng, dm.tq_pad // nq),
        in_specs=[pl.BlockSpec((nq, w), lambda g, i: (i, g)),
                  pl.BlockSpec((dm.tk_pad, w), lambda g, i: (0, g), pipeline_mode=once),
                  pl.BlockSpec((dm.tk_pad // dm.ch, w, dm.ch), lambda g, i: (0, g, 0), pipeline_mode=once),
                  pl.BlockSpec((qb, dm.tk_pad, LANE), lambda g, i: (i, 0, 0))],
        out_specs=pl.BlockSpec((nq, w), lambda g, i: (i, g)),
        out_shape=jax.ShapeDtypeStruct((dm.tq_pad, dm.d_r), BF16),
        scratch_shapes=[pltpu.VMEM((hpg, dm.ch, nq), F32), pltpu.VMEM((hpg, dm.ch, nq), F32)],
        compiler_params=_params("parallel", "arbitrary"),
        name="attend_prompt",
    )(qa, ka, vat, bias)


def _sample_index_kernel(pt_ref, *refs, n_pages, page, ih, ds, n_sel, scale):
    del pt_ref
    cik_refs = refs[:n_pages]
    qi_ref, kin_ref, w_ref, bias_ref, key_ref, kall_ref = refs[n_pages:]
    nq = ds * ih
    n_past = n_pages * page
    q2 = jnp.concatenate([qi_ref[:, h * HEAD:(h + 1) * HEAD] for h in range(ih)], axis=0).astype(BF16)
    w = w_ref[...] * scale
    sel = (lax.broadcasted_iota(I32, (nq, LANE), 0) % ds
           == lax.broadcasted_iota(I32, (nq, LANE), 1) % ds).astype(BF16)
    for p in range(n_pages):
        kall_ref[p * page:(p + 1) * page, :] = cik_refs[p][...].astype(BF16)
    kall_ref[n_past:, :] = jnp.concatenate([kin_ref[...], jnp.zeros((page - ds, HEAD), F32)],
                                           axis=0).astype(BF16)
    s = lax.dot_general(kall_ref[...], q2, _NT, preferred_element_type=F32)
    keys = _to_key(_dot2_rhs(jnp.maximum(s, 0.0) * w, sel))
    key_ref[:n_past, :] = keys[:n_past]
    visible = (lax.broadcasted_iota(I32, (page, LANE), 0) <= lax.broadcasted_iota(I32, (page, LANE), 1) % ds)
    key_ref[n_past:, :] = jnp.where(visible, keys[n_past:], KEY_NEG_INF)

    thr = _kth_largest_replicated(lambda cand: _count_ge(key_ref[...], cand), n_sel, ds)
    bias_ref[...] = _select_bias(key_ref[...], thr).astype(bias_ref.dtype)


def _dot2_rhs(x, b):
    hi = x.astype(BF16)
    lo = (x - hi.astype(F32)).astype(BF16)
    return jnp.dot(hi, b, preferred_element_type=F32) + jnp.dot(lo, b, preferred_element_type=F32)


def _index_select_sample(page_table, cik, qi, ki, w_s, dm):
    assert dm.page == LANE and dm.heads * dm.ds == LANE and dm.t % dm.ds == 0
    nq = dm.ds * dm.ih
    n_keys = (dm.n_pages + 1) * dm.page
    assert n_keys >= dm.nsel_s
    r0 = dm.t // dm.ds
    page_spec = lambda p: pl.BlockSpec((None, dm.page, HEAD), lambda b, pt, p=p: (pt[b, p], 0, 0))
    grid_spec = pltpu.PrefetchScalarGridSpec(
        num_scalar_prefetch=1,
        grid=(dm.db,),
        in_specs=[page_spec(p) for p in range(dm.n_pages)] + [
            pl.BlockSpec((dm.ds, dm.ih * HEAD), lambda b, pt: (r0 + b, 0)),
            pl.BlockSpec((dm.ds, HEAD), lambda b, pt: (r0 + b, 0)),
            pl.BlockSpec((None, 1, nq), lambda b, pt: (b, 0, 0))],
        out_specs=pl.BlockSpec((None, n_keys, LANE), lambda b, pt: (b, 0, 0)),
        scratch_shapes=[pltpu.VMEM((n_keys, LANE), I32), pltpu.VMEM((n_keys, HEAD), BF16)],
    )
    return pl.pallas_call(
        functools.partial(_sample_index_kernel, n_pages=dm.n_pages, page=dm.page, ih=dm.ih, ds=dm.ds,
                          n_sel=dm.nsel_s, scale=dm.ih ** -0.5 * HEAD ** -0.5),
        grid_spec=grid_spec,
        out_shape=jax.ShapeDtypeStruct((dm.db, n_keys, LANE), BF16),
        compiler_params=_params("parallel"),
        name="index_select_sample",
    )(page_table, *([cik] * dm.n_pages), qi, ki, w_s)


def _sample_attn_kernel(pt_ref, *refs, pps, page, nh, ds):
    del pt_ref
    k_refs, v_refs = refs[:pps], refs[pps:2 * pps]
    q_ref, kn_ref, vn_ref, bias_ref, o_ref, qt_ref, onehot_ref, headmask_ref, m_ref, l_ref, acc_ref = refs[2 * pps:]
    b, g = pl.program_id(0), pl.program_id(1)
    rows = page * nh

    @pl.when((b == 0) & (g == 0))
    def _():
        r = lax.broadcasted_iota(I32, (rows, LANE), 0)
        c = lax.broadcasted_iota(I32, (rows, LANE), 1)
        onehot_ref[...] = (r // nh == c).astype(BF16)
        headmask_ref[...] = jnp.where(r % nh == c // ds, 0.0, MASKED)

    @pl.when(g == 0)
    def _():
        q8 = q_ref[...]
        qstack = jnp.concatenate([q8[:, h * HEAD:(h + 1) * HEAD] for h in range(nh)], axis=0)
        qt_ref[...] = qstack.T.astype(BF16)
        m_ref[...] = jnp.full_like(m_ref, -jnp.inf)
        l_ref[...] = jnp.zeros_like(l_ref)
        acc_ref[...] = jnp.zeros_like(acc_ref)

    def attend(k2, v2, onehot, headmask, bias_tile):
        rhs = jnp.concatenate([qt_ref[...], bias_tile], axis=0)
        n = k2.shape[0]
        step = min(n, 512)
        s = jnp.concatenate(
            [jnp.dot(jnp.concatenate([k2[r:r + step].astype(BF16), onehot[r:r + step]], axis=1), rhs,
                     preferred_element_type=F32) + headmask[r:r + step] for r in range(0, n, step)], axis=0)
        m_old = m_ref[...]
        m_new = jnp.maximum(m_old, jnp.max(_fold_rows(s, jnp.maximum), axis=0, keepdims=True))
        alpha = jnp.exp(m_old - m_new)
        p = jnp.exp(s - m_new)
        l_ref[...] = alpha * l_ref[...] + jnp.sum(_fold_rows(p, jnp.add), axis=0, keepdims=True)
        acc_ref[...] = alpha * acc_ref[...] + lax.dot_general(v2.astype(BF16), p.astype(BF16), _TN,
                                                              preferred_element_type=F32)
        m_ref[...] = m_new

    for p in range(pps):
        r0 = pl.multiple_of((g * pps + p) * page, page)
        attend(k_refs[p][...], v_refs[p][...], onehot_ref[...], headmask_ref[...], bias_ref[pl.ds(r0, page), :])

    @pl.when(g == pl.num_programs(1) - 1)
    def _():
        n_past = bias_ref.shape[0] - page
        by_head = lambda ref: jnp.concatenate([ref[:, h * HEAD:(h + 1) * HEAD] for h in range(nh)], axis=0)
        r = lax.broadcasted_iota(I32, (nh * ds, LANE), 0)
        c = lax.broadcasted_iota(I32, (nh * ds, LANE), 1)
        attend(by_head(kn_ref), by_head(vn_ref), (r % ds == c).astype(BF16),
               jnp.where(r // ds == c // ds, 0.0, MASKED), bias_ref[n_past:, :])
        out = (acc_ref[...] / l_ref[...]).T
        for h in range(nh):
            o_ref[:, h * HEAD:(h + 1) * HEAD] = out[h * ds:(h + 1) * ds, :]


def _attend_sample(page_table, ck, cv, qa, k_new, v_new, bias, dm):
    assert dm.page == LANE and dm.heads * dm.ds == LANE
    pps = 4 if dm.n_pages % 4 == 0 else 1
    w = dm.heads * HEAD
    rows = dm.page * dm.heads
    n_keys = (dm.n_pages + 1) * dm.page
    r0 = dm.t // dm.ds
    page_spec = lambda p: pl.BlockSpec((None, rows, HEAD), lambda b, g, pt, p=p: (pt[b, g * pps + p], 0, 0))
    grid_spec = pltpu.PrefetchScalarGridSpec(
        num_scalar_prefetch=1,
        grid=(dm.db, dm.n_pages // pps),
        in_specs=[page_spec(p) for p in range(pps)] * 2 + [
            pl.BlockSpec((dm.ds, w), lambda b, g, pt: (r0 + b, 0)),
            pl.BlockSpec((dm.ds, w), lambda b, g, pt: (b, 0)),
            pl.BlockSpec((dm.ds, w), lambda b, g, pt: (b, 0)),
            pl.BlockSpec((None, n_keys, LANE), lambda b, g, pt: (b, 0, 0))],
        out_specs=pl.BlockSpec((dm.ds, w), lambda b, g, pt: (b, 0)),
        scratch_shapes=[pltpu.VMEM((HEAD, LANE), BF16), pltpu.VMEM((rows, LANE), BF16),
                        pltpu.VMEM((rows, LANE), F32), pltpu.VMEM((1, LANE), F32), pltpu.VMEM((1, LANE), F32),
                        pltpu.VMEM((HEAD, LANE), F32)],
    )
    n_pool = ck.shape[0]
    return pl.pallas_call(
        functools.partial(_sample_attn_kernel, pps=pps, page=dm.page, nh=dm.heads, ds=dm.ds),
        grid_spec=grid_spec,
        out_shape=jax.ShapeDtypeStruct((dm.db * dm.ds, w), F32),
        compiler_params=_params("arbitrary", "arbitrary"),
        name="attend_sample",
    )(page_table, *([ck.reshape(n_pool, rows, HEAD)] * pps), *([cv.reshape(n_pool, rows, HEAD)] * pps),
      qa, k_new, v_new, bias)


def kernel(x_prompt, x_sample, cache_k, cache_v, cache_idx_k, state_hgrn, page_table, meta_tokens, lb_logits,
           norm_mix_g, w_in, q_norm_g, w_uq, w_uq_idx, idx_k_norm_g, idx_k_norm_b, hgrn_norm_g, w_out,
           norm_ffn_g, w_gate, w_up, w_down, final_norm_g):
    assert x_prompt.shape[0] == 1 and w_in.shape[0] == 1, "single prompt sequence, single layer"
    dm = _dims(x_prompt, x_sample, cache_k, page_table)
    d, t, r, rp = dm.d, dm.t, dm.r, dm.rp
    ns = dm.db * dm.ds
    off_cq = 4 * dm.d_r
    off_k = off_cq + dm.q_rank
    off_v = off_k + dm.d_r
    off_ik = off_v + dm.d_r

    h0 = jnp.concatenate([meta_tokens.astype(F32), x_prompt[0], x_sample.reshape(ns, d),
                          jnp.zeros((rp - r, d), F32)], axis=0)
    hn = _rmsnorm(h0, norm_mix_g[0], width=d, col_block=0, out_dtype=BF16)
    z = _matmul(hn, w_in[0], dm, out_dtype=F32, n_cols=off_ik, name="in_proj")
    w_tail = jnp.pad(w_in[0][:, off_ik:], ((0, 0), (0, 2 * HEAD - (dm.d_in - off_ik))))
    z_tail = _matmul(hn, w_tail, dm, out_dtype=F32, name="in_proj_idx")

    k_s = z[t:r, off_k:off_k + dm.d_r]
    v_s = z[t:r, off_v:off_v + dm.d_r]

    o_r_p, s_p = _hgrn_prompt(z, lb_logits, hgrn_norm_g[0], dm)
    o_r_s, s_s = _hgrn_sample(z, lb_logits, hgrn_norm_g[0], state_hgrn[0], dm)

    cqn = _rmsnorm(z, q_norm_g[0], width=dm.q_rank, col_block=off_cq // dm.q_rank, out_dtype=BF16)
    qa = _matmul(cqn, w_uq[0], dm, out_dtype=F32, scale=HEAD ** -0.5, name="q_up")
    qi = _matmul(cqn, w_uq_idx[0], dm, out_dtype=F32, name="q_idx_up")
    ki, ki_b, wt = _idx_prep(z_tail, idx_k_norm_g[0], idx_k_norm_b[0], dm, 0)

    ka, vat = _attn_operands(z, off_k, off_v, dm)
    bias_p = _index_select_prompt(qi, ki_b, wt, dm)
    o_a_p = _attend_prompt(qa, ka, vat, bias_p, dm)

    w_s = z_tail[t:r, HEAD:HEAD + dm.ih].reshape(dm.db, dm.ds, dm.ih).swapaxes(1, 2).reshape(dm.db, 1, dm.ds * dm.ih)
    bias_s = _index_select_sample(page_table, cache_idx_k[0], qi, ki, w_s, dm)
    o_a_s = _attend_sample(page_table, cache_k[0], cache_v[0], qa, k_s, v_s, bias_s, dm)

    o_mix = jnp.concatenate([
        jnp.concatenate([o_r_p[:t], o_a_p[:t]], axis=1),
        jnp.concatenate([o_r_s, o_a_s], axis=1).astype(BF16),
        jnp.zeros((rp - r, 2 * dm.d_r), BF16)], axis=0)
    h1 = _matmul(o_mix, w_out[0], dm, out_dtype=F32, residual=h0, name="out_proj")

    hf = _rmsnorm(h1, norm_ffn_g[0], width=d, col_block=0, out_dtype=BF16)
    act = _ffn_up(hf, w_gate[0], w_up[0], dm)
    h2 = _matmul(act, w_down[0].astype(BF16), dm, out_dtype=F32, residual=h1, tn_max=256, name="ffn_down")
    y_prompt = _rmsnorm_rows(h2, final_norm_g, N_META, t - N_META)
    y_sample = _rmsnorm_rows(h2, final_norm_g, t, ns)

    sample_heads = lambda a: a.reshape(1, dm.db, dm.ds, dm.heads, HEAD)
    return (y_prompt[None],
            y_sample.reshape(dm.db, dm.ds, d),
            _prompt_heads(z, off_k, dm)[None, None], _prompt_heads(z, off_v, dm)[None, None],
            ki[:t].reshape(1, 1, t, HEAD), s_p[None, None],
            sample_heads(k_s), sample_heads(v_s), ki[t:r].reshape(1, dm.db, dm.ds, HEAD), s_s[None])
```

```python
import functools
from typing import NamedTuple

import jax
import jax.numpy as jnp
from jax import lax
from jax.experimental import pallas as pl
from jax.experimental.pallas import tpu as pltpu

F32 = jnp.float32
BF16 = jnp.bfloat16
I32 = jnp.int32

EPS = 1e-6
N_META = 16
HEAD = 128
TOPK_MAX = 256
LANE = 128
VMEM_LIMIT_BYTES = 56 * 2**20
QUERY_BLOCKS = 2
MASKED = -1e30
INT_MIN = -2**31
KEY_NEG_INF = -2139095041

_NT = (((1,), (1,)), ((), ()))
_TN = (((0,), (0,)), ((), ()))


class Dims(NamedTuple):
    d: int
    t: int
    t_pad: int
    tq_pad: int
    tk_pad: int
    db: int
    ds: int
    r: int
    rp: int
    tm: int
    heads: int
    ih: int
    d_r: int
    q_rank: int
    d_in: int
    d_ff: int
    n_pages: int
    page: int
    nsel_p: int
    nsel_s: int
    cb: int
    ch: int


def _round_up(x, m):
    return -(-x // m) * m


def _dims(x_prompt, x_sample, cache_k, page_table):
    d = x_prompt.shape[-1]
    seq = x_prompt.shape[1]
    db, ds = x_sample.shape[:2]
    t = seq + N_META
    t_pad = _round_up(t, LANE)
    nb = t_pad // LANE
    cb = 5 if nb % 5 == 0 else 1
    tq_pad = _round_up(t, QUERY_BLOCKS * LANE)
    tk_pad = _round_up(t_pad, 2 * cb * LANE)
    r = t + db * ds
    rp = max(_round_up(r, LANE), tq_pad, tk_pad)
    n_tiles = max(1, -(-rp // 1200))
    while (rp // 16) % n_tiles:
        n_tiles += 1
    heads = d // 256
    d_r = heads * HEAD
    q_rank = d // 4
    ih = d // 128
    d_in = 6 * d_r + q_rank + HEAD + ih
    d_ff = -(-8 * d // (3 * 256)) * 256
    n_pages, page = page_table.shape[1], cache_k.shape[2]
    past = n_pages * page
    return Dims(d=d, t=t, t_pad=t_pad, tq_pad=tq_pad, tk_pad=tk_pad, db=db, ds=ds, r=r, rp=rp, tm=rp // n_tiles, heads=heads, ih=ih,
                d_r=d_r, q_rank=q_rank, d_in=d_in, d_ff=d_ff,
                n_pages=n_pages, page=page,
                nsel_p=min(TOPK_MAX, seq // 4), nsel_s=min(TOPK_MAX, (past + ds) // 4),
                cb=cb, ch=cb * LANE)


def _tile(n, pref):
    best = LANE
    for c in range(LANE, min(n, pref) + 1, LANE):
        if n % c == 0:
            best = c
    return best


def _params(*sem):
    return pltpu.CompilerParams(dimension_semantics=sem, vmem_limit_bytes=VMEM_LIMIT_BYTES)


def _sigmoid(x):
    return 1.0 / (1.0 + jnp.exp(-x))


def _silu(x):
    return x * _sigmoid(x)


def _rmsnorm_kernel(x_ref, g_ref, o_ref):
    x = x_ref[...]
    ms = jnp.mean(x * x, axis=-1, keepdims=True)
    o_ref[...] = (x * lax.rsqrt(ms + EPS) * g_ref[...]).astype(o_ref.dtype)


def _rmsnorm(x, g, *, width, col_block, out_dtype):
    rows = x.shape[0]
    tr = LANE
    return pl.pallas_call(
        _rmsnorm_kernel,
        grid=(rows // tr,),
        in_specs=[pl.BlockSpec((tr, width), lambda i: (i, col_block)),
                  pl.BlockSpec((1, width), lambda i: (0, 0))],
        out_specs=pl.BlockSpec((tr, width), lambda i: (i, 0)),
        out_shape=jax.ShapeDtypeStruct((rows, width), out_dtype),
        compiler_params=_params("parallel"),
        name="rmsnorm",
    )(x, g.reshape(1, width).astype(F32))


def _rmsnorm_rows(x, g, row0, n_rows):
    width = x.shape[1]
    tr = max(c for c in range(8, LANE + 1, 8) if n_rows % c == 0)
    assert row0 % 8 == 0
    return pl.pallas_call(
        _rmsnorm_kernel,
        grid=(n_rows // tr,),
        in_specs=[pl.BlockSpec((pl.Element(tr), pl.Element(width)), lambda i: (pl.multiple_of(row0 + i * tr, 8), 0)),
                  pl.BlockSpec((1, width), lambda i: (0, 0))],
        out_specs=pl.BlockSpec((tr, width), lambda i: (i, 0)),
        out_shape=jax.ShapeDtypeStruct((n_rows, width), F32),
        compiler_params=_params("parallel"),
        name="rmsnorm_rows",
    )(x, g.reshape(1, width).astype(F32))


def _idx_prep_kernel(ik_ref, iw_ref, g_ref, b_ref, ki_ref, kib_ref, wt_ref, *, ih, scale):
    x = ik_ref[...]
    mu = jnp.mean(x, axis=-1, keepdims=True)
    xc = x - mu
    var = jnp.mean(xc * xc, axis=-1, keepdims=True)
    ki = xc * lax.rsqrt(var + EPS) * g_ref[...] + b_ref[...]
    ki_ref[...] = ki
    kib_ref[...] = ki.astype(BF16)
    wt_ref[...] = (iw_ref[...] * scale).T[:ih, :]


def _idx_prep(z, g, b, dm, off_ik):
    tr = LANE
    cb = off_ik // HEAD
    return pl.pallas_call(
        functools.partial(_idx_prep_kernel, ih=dm.ih, scale=dm.ih ** -0.5 * HEAD ** -0.5),
        grid=(dm.rp // tr,),
        in_specs=[pl.BlockSpec((tr, HEAD), lambda i: (i, cb)),
                  pl.BlockSpec((tr, HEAD), lambda i: (i, cb + 1)),
                  pl.BlockSpec((1, HEAD), lambda i: (0, 0)),
                  pl.BlockSpec((1, HEAD), lambda i: (0, 0))],
        out_specs=[pl.BlockSpec((tr, HEAD), lambda i: (i, 0)),
                   pl.BlockSpec((tr, HEAD), lambda i: (i, 0)),
                   pl.BlockSpec((dm.ih, tr), lambda i: (0, i))],
        out_shape=[jax.ShapeDtypeStruct((dm.rp, HEAD), F32),
                   jax.ShapeDtypeStruct((dm.rp, HEAD), BF16),
                   jax.ShapeDtypeStruct((dm.ih, dm.rp), F32)],
        compiler_params=_params("parallel"),
        name="idx_prep",
    )(z, z, g.reshape(1, HEAD), b.reshape(1, HEAD))


def _heads_out_kernel(x0_ref, x1_ref, o_ref, *, nh):
    half = nh // 2
    for h in range(nh):
        src = x0_ref if h < half else x1_ref
        o_ref[:, h, :] = src[:, (h % half) * HEAD:(h % half + 1) * HEAD]


def _prompt_heads(z, off, dm):
    wblk = dm.d_r // 2
    tr = max(c for c in range(8, 513, 8) if dm.t % c == 0)
    cb = off // wblk
    assert off % wblk == 0 and dm.heads % 2 == 0
    return pl.pallas_call(
        functools.partial(_heads_out_kernel, nh=dm.heads),
        grid=(dm.t // tr,),
        in_specs=[pl.BlockSpec((tr, wblk), lambda i: (i, cb)),
                  pl.BlockSpec((tr, wblk), lambda i: (i, cb + 1))],
        out_specs=pl.BlockSpec((tr, dm.heads, HEAD), lambda i: (i, 0, 0)),
        out_shape=jax.ShapeDtypeStruct((dm.t, dm.heads, HEAD), F32),
        compiler_params=_params("parallel"),
        name="prompt_heads",
    )(z, z)


def _attn_operands_kernel(k0_ref, k1_ref, v0_ref, v1_ref, ka_ref, vat_ref):
    w = k0_ref.shape[1]
    ka_ref[:, :w] = k0_ref[...].astype(BF16)
    ka_ref[:, w:] = k1_ref[...].astype(BF16)
    for c0 in range(0, w, LANE):
        vat_ref[c0:c0 + LANE, :] = v0_ref[:, c0:c0 + LANE].T.astype(BF16)
        vat_ref[w + c0:w + c0 + LANE, :] = v1_ref[:, c0:c0 + LANE].T.astype(BF16)


def _attn_operands(z, off_k, off_v, dm):
    wblk = dm.d_r // 2
    ck, cv = off_k // wblk, off_v // wblk
    assert off_k % wblk == 0 and off_v % wblk == 0
    blk = lambda c: pl.BlockSpec((LANE, wblk), lambda i, c=c: (i, c))
    return pl.pallas_call(
        _attn_operands_kernel,
        grid=(dm.tk_pad // LANE,),
        in_specs=[blk(ck), blk(ck + 1), blk(cv), blk(cv + 1)],
        out_specs=[pl.BlockSpec((LANE, dm.d_r), lambda i: (i, 0)),
                   pl.BlockSpec((None, dm.d_r, LANE), lambda i: (i // dm.cb, 0, i % dm.cb))],
        out_shape=[jax.ShapeDtypeStruct((dm.tk_pad, dm.d_r), BF16),
                   jax.ShapeDtypeStruct((dm.tk_pad // dm.ch, dm.d_r, dm.ch), BF16)],
        compiler_params=_params("parallel"),
        name="attn_operands",
    )(z, z, z, z)


def _weight_tile(w_ref):
    w = w_ref[...]
    return w if w.dtype == BF16 else w.astype(BF16)


def _mm_kernel(a_ref, b_ref, o_ref, *, scale):
    acc = jnp.dot(a_ref[...], _weight_tile(b_ref), preferred_element_type=F32)
    if scale != 1.0:
        acc = acc * scale
    o_ref[...] = acc.astype(o_ref.dtype)


def _mm_res_kernel(a_ref, b_ref, r_ref, o_ref):
    o_ref[...] = r_ref[...] + jnp.dot(a_ref[...], _weight_tile(b_ref), preferred_element_type=F32)


def _matmul(a, b, dm, *, out_dtype, name, scale=1.0, residual=None, n_cols=None, tn_max=512):
    m, k = a.shape
    n = b.shape[1] if n_cols is None else n_cols
    tm, tn = dm.tm, _tile(n, tn_max)
    a_mode = {"pipeline_mode": pl.Buffered(1)} if tm * k * a.dtype.itemsize > 16 * 2**20 else {}
    in_specs = [pl.BlockSpec((tm, k), lambda i, j: (i, 0), **a_mode),
                pl.BlockSpec((k, tn), lambda i, j: (0, j))]
    args = [a, b]
    if residual is None:
        body = functools.partial(_mm_kernel, scale=scale)
    else:
        body = _mm_res_kernel
        in_specs.append(pl.BlockSpec((tm, tn), lambda i, j: (i, j)))
        args.append(residual)
    return pl.pallas_call(
        body,
        grid=(m // tm, n // tn),
        in_specs=in_specs,
        out_specs=pl.BlockSpec((tm, tn), lambda i, j: (i, j)),
        out_shape=jax.ShapeDtypeStruct((m, n), out_dtype),
        compiler_params=_params("parallel", "parallel"),
        name=name,
    )(*args)


def _mm_nt_kernel(a_ref, bt_ref, o_ref):
    o_ref[...] = lax.dot_general(a_ref[...], _weight_tile(bt_ref), _NT,
                                 preferred_element_type=F32).astype(o_ref.dtype)


def _matmul_nt(a, bt, dm, *, out_dtype, name, n_cols=None):
    m, k = a.shape
    n = bt.shape[0] if n_cols is None else n_cols
    tm, tn = dm.tm, _tile(n, 512)
    return pl.pallas_call(
        _mm_nt_kernel,
        grid=(m // tm, n // tn),
        in_specs=[pl.BlockSpec((tm, k), lambda i, j: (i, 0)),
                  pl.BlockSpec((tn, k), lambda i, j: (j, 0))],
        out_specs=pl.BlockSpec((tm, tn), lambda i, j: (i, j)),
        out_shape=jax.ShapeDtypeStruct((m, n), out_dtype),
        compiler_params=_params("parallel", "parallel"),
        name=name,
    )(a, bt)


def _ffn_up_kernel(x_ref, wg_ref, wu_ref, o_ref):
    x = x_ref[...]
    g = jnp.dot(x, _weight_tile(wg_ref), preferred_element_type=F32)
    u = jnp.dot(x, _weight_tile(wu_ref), preferred_element_type=F32)
    o_ref[...] = (_silu(g) * u).astype(o_ref.dtype)


def _ffn_up(x, wg, wu, dm):
    m, k = x.shape
    n = wg.shape[1]
    tm, tn = dm.tm, _tile(n, 256)
    return pl.pallas_call(
        _ffn_up_kernel,
        grid=(m // tm, n // tn),
        in_specs=[pl.BlockSpec((tm, k), lambda i, j: (i, 0)),
                  pl.BlockSpec((k, tn), lambda i, j: (0, j)),
                  pl.BlockSpec((k, tn), lambda i, j: (0, j))],
        out_specs=pl.BlockSpec((tm, tn), lambda i, j: (i, j)),
        out_shape=jax.ShapeDtypeStruct((m, n), BF16),
        compiler_params=_params("parallel", "parallel"),
        name="ffn_up",
    )(x, wg, wu)


def _split3(x):
    hi = x.astype(BF16)
    r = x - hi.astype(F32)
    mid = r.astype(BF16)
    lo = (r - mid.astype(F32)).astype(BF16)
    return hi, mid, lo


def _dot3(a, x):
    hi, mid, lo = _split3(x)
    return (jnp.dot(a, hi, preferred_element_type=F32) + jnp.dot(a, mid, preferred_element_type=F32)
            + jnp.dot(a, lo, preferred_element_type=F32))


def _lower_bound(lbl):
    e = jnp.exp(lbl - jnp.max(lbl, axis=0, keepdims=True))
    return e[0:1, :] / jnp.sum(e, axis=0, keepdims=True)


def _hgrn_gates(q_r, f_r, lb):
    key = (1.0 - lb) * _sigmoid(-f_r)
    return _silu(q_r), key, jnp.log(1.0 - key)


def _diag_block(q, key, g, v, ones):
    n = q.shape[0]
    rows = lax.broadcasted_iota(I32, (n, HEAD), 0)
    parts = []
    for s in range(n):
        lo = s // 8 * 8
        decay = jnp.exp(jnp.minimum(g[lo:] - g[s:s + 1, :], 0.0))
        live = jnp.where(rows[lo:] >= s, q[lo:] * (key[s:s + 1, :] * decay), 0.0)
        parts.append(jnp.concatenate([jnp.zeros((lo, HEAD), F32), live], axis=0) if lo else live)
    p = jnp.concatenate(parts, axis=0).astype(BF16)
    a = jnp.dot(p, ones, preferred_element_type=F32)
    o_groups = [None] * (n // 8)
    for s in range(n):
        for gi in range(s // 8, n // 8):
            term = a[s * n + gi * 8:s * n + gi * 8 + 8] * v[s:s + 1, :]
            o_groups[gi] = term if o_groups[gi] is None else o_groups[gi] + term
    return jnp.concatenate(o_groups, axis=0) if len(o_groups) > 1 else o_groups[0]


def _hgrn_levels(q, key, g):
    c = q.shape[0]
    row = lax.broadcasted_iota(I32, (c, HEAD), 0)
    t_i = lax.broadcasted_iota(I32, (c, c), 0)
    s_i = lax.broadcasted_iota(I32, (c, c), 1)
    total = None
    b = c // 2
    while b >= 16:
        gref = g[b - 1:b, :]
        for m in range(2 * b, c, 2 * b):
            gref = jnp.where(row >= m, g[m + b - 1:m + b, :], gref)
        lower = (row & b) != 0
        qs = jnp.where(lower, q * jnp.exp(jnp.minimum(g - gref, 0.0)), 0.0)
        ks = jnp.where(lower, 0.0, key * jnp.exp(jnp.minimum(gref - g, 0.0)))
        a = lax.dot_general(qs.astype(BF16), ks.astype(BF16), _NT, preferred_element_type=F32)
        if 2 * b < c:
            a = jnp.where((t_i // (2 * b)) == (s_i // (2 * b)), a, 0.0)
        total = a if total is None else total + a
        b //= 2
    return total


def _scan_rows8(x):
    row = lax.broadcasted_iota(I32, x.shape, 0)
    for sh in (1, 2, 4):
        x = x + jnp.where(row >= sh, pltpu.roll(x, sh, 0), 0.0)
    return x


def _hgrn_out(o, g_norm, og):
    ms = jnp.mean(o * o, axis=-1, keepdims=True)
    return o * lax.rsqrt(ms + EPS) * g_norm * _silu(og)


def _hgrn_prompt_kernel(q_ref, f_ref, i_ref, og_ref, lbl_ref, gn_ref, o_ref, s_ref, st_ref, *, t, hpg):
    c = pl.program_id(1)

    @pl.when(c == 0)
    def _():
        st_ref[...] = jnp.zeros_like(st_ref)

    valid = (c * LANE + lax.broadcasted_iota(I32, (LANE, HEAD), 0)) < t
    tri = (lax.broadcasted_iota(I32, (LANE, LANE), 0) >= lax.broadcasted_iota(I32, (LANE, LANE), 1)).astype(BF16)
    ones = jnp.ones((HEAD, HEAD), BF16)
    for h in range(hpg):
        sl = slice(h * HEAD, (h + 1) * HEAD)
        q, key, lf = _hgrn_gates(q_ref[:, sl], f_ref[:, sl], _lower_bound(lbl_ref[:, sl]))
        v = i_ref[:, sl]
        key = jnp.where(valid, key, 0.0)
        lf = jnp.where(valid, lf, 0.0)
        g = _dot3(tri, lf)
        g_last = g[LANE - 1:LANE, :]
        st = st_ref[h]
        vb = v.astype(BF16)
        o = lax.dot_general((q * jnp.exp(g)).astype(BF16), st.astype(BF16), _NT, preferred_element_type=F32)
        o = o + jnp.dot(_hgrn_levels(q, key, g).astype(BF16), vb, preferred_element_type=F32)
        o = o + jnp.concatenate(
            [_diag_block(q[m:m + 16], key[m:m + 16], g[m:m + 16], v[m:m + 16], ones) for m in range(0, LANE, 16)],
            axis=0)
        kd = (key * jnp.exp(g_last - g)).astype(BF16)
        st_new = jnp.exp(g_last) * st + lax.dot_general(vb, kd, _TN, preferred_element_type=F32)
        st_ref[h] = st_new
        o_ref[:, sl] = _hgrn_out(o, gn_ref[...], og_ref[:, sl]).astype(o_ref.dtype)

    @pl.when(c == pl.num_programs(1) - 1)
    def _():
        for h in range(hpg):
            s_ref[h] = st_ref[h].T


def _hgrn_prompt(z, lb_logits, g_norm, dm):
    hpg = 2 if dm.heads % 2 == 0 else 1
    ng, nc, w = dm.heads // hpg, dm.t_pad // LANE, hpg * HEAD
    blk = lambda off: pl.BlockSpec((LANE, w), lambda g, c, off=off: (c, off * ng + g))
    nl = lb_logits.shape[0]
    return pl.pallas_call(
        functools.partial(_hgrn_prompt_kernel, t=dm.t, hpg=hpg),
        grid=(ng, nc),
        in_specs=[blk(0), blk(1), blk(2), blk(3),
                  pl.BlockSpec((nl, w), lambda g, c: (0, g)),
                  pl.BlockSpec((1, HEAD), lambda g, c: (0, 0))],
        out_specs=[pl.BlockSpec((LANE, w), lambda g, c: (c, g)),
                   pl.BlockSpec((hpg, HEAD, HEAD), lambda g, c: (g, 0, 0))],
        out_shape=[jax.ShapeDtypeStruct((dm.t_pad, dm.d_r), BF16),
                   jax.ShapeDtypeStruct((dm.heads, HEAD, HEAD), F32)],
        scratch_shapes=[pltpu.VMEM((hpg, HEAD, HEAD), F32)],
        compiler_params=_params("parallel", "arbitrary"),
        name="hgrn_prompt",
    )(z, z, z, z, lb_logits, g_norm.reshape(1, HEAD))


def _hgrn_sample_kernel(q_ref, f_ref, i_ref, og_ref, lbl_ref, gn_ref, s0_ref, o_ref, s_ref, *, nh, ds):
    q, key, lf = _hgrn_gates(q_ref[...], f_ref[...], _lower_bound(lbl_ref[...]))
    v = i_ref[...]
    g = _scan_rows8(lf)
    g_last = g[ds - 1:ds, :]
    qe = q * jnp.exp(g)
    kd = key * jnp.exp(g_last - g)
    e_last = jnp.exp(g_last)
    ones = jnp.ones((HEAD, HEAD), BF16)
    zpad = jnp.zeros((LANE - ds, HEAD), F32)
    og = og_ref[...]
    gn = gn_ref[...]
    for h in range(nh):
        sl = slice(h * HEAD, (h + 1) * HEAD)
        st = s0_ref[h].T
        o = lax.dot_general(qe[:, sl].astype(BF16), st.astype(BF16), _NT, preferred_element_type=F32)
        o = o + _diag_block(q[:, sl], key[:, sl], g[:, sl], v[:, sl], ones)
        vp = jnp.concatenate([v[:, sl], zpad], axis=0).astype(BF16)
        kp = jnp.concatenate([kd[:, sl], zpad], axis=0).astype(BF16)
        st_new = e_last[:, sl] * st + lax.dot_general(vp, kp, _TN, preferred_element_type=F32)
        s_ref[h] = st_new.T
        o_ref[:, sl] = _hgrn_out(o, gn, og[:, sl])


def _hgrn_sample(z, lb_logits, g_norm, state, dm):
    assert dm.ds == 8 and dm.t % dm.ds == 0
    nh, r0 = dm.heads, dm.t // dm.ds
    blk = lambda off: pl.BlockSpec((dm.ds, dm.d_r), lambda b, off=off: (r0 + b, off))
    nl = lb_logits.shape[0]
    return pl.pallas_call(
        functools.partial(_hgrn_sample_kernel, nh=nh, ds=dm.ds),
        grid=(dm.db,),
        in_specs=[blk(0), blk(1), blk(2), blk(3),
                  pl.BlockSpec((nl, dm.d_r), lambda b: (0, 0)),
                  pl.BlockSpec((1, HEAD), lambda b: (0, 0)),
                  pl.BlockSpec((None, nh, HEAD, HEAD), lambda b: (b, 0, 0, 0))],
        out_specs=[pl.BlockSpec((dm.ds, dm.d_r), lambda b: (b, 0)),
                   pl.BlockSpec((None, nh, HEAD, HEAD), lambda b: (b, 0, 0, 0))],
        out_shape=[jax.ShapeDtypeStruct((dm.db * dm.ds, dm.d_r), F32),
                   jax.ShapeDtypeStruct((dm.db, nh, HEAD, HEAD), F32)],
        compiler_params=_params("parallel"),
        name="hgrn_sample",
    )(z, z, z, z, lb_logits, g_norm.reshape(1, HEAD), state)


def _to_key(x):
    b = lax.bitcast_convert_type(x, I32)
    return b ^ (lax.shift_right_arithmetic(b, 31) & 0x7FFFFFFF)


def _fold_rows(x, op):
    total = None
    for r0 in range(0, x.shape[0], 256):
        parts = [x[r:r + 8] for r in range(r0, min(r0 + 256, x.shape[0]), 8)]
        while len(parts) > 1:
            parts = [op(parts[j], parts[j + 1]) for j in range(0, len(parts) - 1, 2)] + parts[len(parts) & ~1:]
        total = parts[0] if total is None else op(total, parts[0])
    return total


def _count_ge(keys, cand):
    return _fold_rows((keys >= cand).astype(I32), jnp.add)


def _kth_largest(count_ge, n_sel):
    def body(it, tp):
        bit = jnp.left_shift(jnp.int32(1), 31 - it)
        cand = tp | bit
        cnt = jnp.sum(count_ge(cand ^ INT_MIN), axis=0, keepdims=True)
        return jnp.where(cnt >= n_sel, cand, tp)
    return lax.fori_loop(0, 32, body, jnp.zeros((1, LANE), I32)) ^ INT_MIN


def _kth_largest_replicated(count_ge, n_sel, period):
    copies = LANE // period
    bits = copies.bit_length() - 1
    assert copies == 1 << bits and 32 % bits == 0
    digit = lax.broadcasted_iota(I32, (8, LANE), 1) // period

    def body(it, tp):
        shift = 32 - bits * (it + 1)
        cand = tp | jnp.left_shift(digit, shift)
        cnt = jnp.sum(count_ge(cand[0:1] ^ INT_MIN), axis=0, keepdims=True)
        ok = jnp.broadcast_to((cnt >= n_sel).astype(I32), (8, LANE))
        passed = _fold_rows(jnp.concatenate([ok] + [pltpu.roll(ok, j * period, 1) for j in range(1, copies)],
                                            axis=0), jnp.add)
        return tp | jnp.left_shift(passed - 1, shift)

    tp = lax.fori_loop(0, 32 // bits, body, jnp.zeros((8, LANE), I32))
    return tp[0:1] ^ INT_MIN


def _select_bias(keys, thr):
    return jnp.where((keys >= thr) & (keys > KEY_NEG_INF), 0.0, MASKED)


def _index_kernel(qi_ref, ki_ref, wt_ref, bias_ref, key_ref, qt_ref, *, ih, ch, cb, n_sel):
    i = pl.program_id(0)
    n_tiles = jnp.minimum(i + 1, key_ref.shape[0] // LANE)
    n_ch = (n_tiles + cb - 1) // cb
    n_total = bias_ref.shape[0] // ch
    for hp in range(ih // 2):
        pair = [qi_ref[:, (2 * hp + e) * HEAD:(2 * hp + e + 1) * HEAD].T for e in (0, 1)]
        qt_ref[hp] = jnp.concatenate(pair, axis=1).astype(BF16)
    rows = 2 * LANE
    q_pos = i * LANE + lax.broadcasted_iota(I32, (rows, LANE), 1)
    k_off = lax.broadcasted_iota(I32, (rows, LANE), 0)
    n_pairs = (n_tiles + 1) // 2

    def score_tiles(c, carry):
        r0 = pl.multiple_of(jnp.minimum(c * rows, key_ref.shape[0] - rows), LANE)
        kc = ki_ref[pl.ds(r0, rows), :]
        acc = jnp.zeros((rows, LANE), F32)
        for hp in range(ih // 2):
            s = jnp.dot(kc, qt_ref[hp], preferred_element_type=F32)
            acc = acc + jnp.maximum(s[:, :LANE], 0.0) * wt_ref[pl.ds(2 * hp, 1), :]
            acc = acc + jnp.maximum(s[:, LANE:], 0.0) * wt_ref[pl.ds(2 * hp + 1, 1), :]
        key_ref[pl.ds(r0, rows), :] = jnp.where(r0 + k_off <= q_pos, _to_key(acc), KEY_NEG_INF)
        return carry

    lax.fori_loop(0, n_pairs, score_tiles, 0)

    def fill_tile(c, carry):
        key_ref[pl.ds(pl.multiple_of(c * LANE, LANE), LANE), :] = jnp.full((LANE, LANE), KEY_NEG_INF, I32)
        return carry

    lax.fori_loop(2 * n_pairs, n_ch * cb, fill_tile, 0)

    def count_ge(cand):
        def body(c, cnt):
            return cnt + _count_ge(key_ref[pl.ds(pl.multiple_of(c * ch, ch), ch), :], cand)
        return lax.fori_loop(0, n_ch, body, jnp.zeros((8, LANE), I32))

    thr = _kth_largest(count_ge, n_sel)

    def write_bias(c, carry):
        r0 = pl.multiple_of(c * ch, ch)
        bias_ref[pl.ds(r0, ch), :] = _select_bias(key_ref[pl.ds(r0, ch), :], thr).astype(bias_ref.dtype)
        return carry

    lax.fori_loop(0, n_ch, write_bias, 0)

    def write_masked(c, carry):
        bias_ref[pl.ds(pl.multiple_of(c * ch, ch), ch), :] = jnp.full((ch, LANE), MASKED, bias_ref.dtype)
        return carry

    lax.fori_loop(n_ch, n_total, write_masked, 0)


def _index_select_prompt(qi, ki_b, wt, dm):
    nb = dm.tq_pad // LANE
    assert dm.ch >= dm.nsel_p and dm.ih % 2 == 0
    return pl.pallas_call(
        functools.partial(_index_kernel, ih=dm.ih, ch=dm.ch, cb=dm.cb, n_sel=dm.nsel_p),
        grid=(nb,),
        in_specs=[pl.BlockSpec((LANE, dm.ih * HEAD), lambda i: (i, 0)),
                  pl.BlockSpec((dm.t_pad, HEAD), lambda i: (0, 0)),
                  pl.BlockSpec((dm.ih, LANE), lambda i: (0, i))],
        out_specs=pl.BlockSpec((None, dm.tk_pad, LANE), lambda i: (i, 0, 0)),
        out_shape=jax.ShapeDtypeStruct((nb, dm.tk_pad, LANE), BF16),
        scratch_shapes=[pltpu.VMEM((dm.t_pad, LANE), I32), pltpu.VMEM((dm.ih // 2, HEAD, 2 * LANE), BF16)],
        compiler_params=_params("parallel"),
        name="index_select_prompt",
    )(qi, ki_b, wt)


def _attn_kernel(q_ref, k_ref, vt_ref, bias_ref, o_ref, sa_ref, sb_ref, *, hpg, qb, ch, cb, t_tiles):
    i = pl.program_id(1)
    n_ch = (jnp.minimum(qb * (i + 1), t_tiles) + cb - 1) // cb
    n_total = k_ref.shape[0] // ch
    assert n_total % 2 == 0
    nq = qb * LANE
    qts = [jnp.concatenate([q_ref[a * LANE:(a + 1) * LANE, h * HEAD:(h + 1) * HEAD].T for a in range(qb)],
                           axis=1).astype(BF16) for h in range(hpg)]

    def put_scores(c, s_ref):
        r0 = pl.multiple_of(c * ch, ch)
        bias = jnp.concatenate([bias_ref[a, pl.ds(r0, ch), :] for a in range(qb)], axis=1).astype(F32)
        for h in range(hpg):
            s_ref[h] = jnp.dot(k_ref[pl.ds(r0, ch), h * HEAD:(h + 1) * HEAD], qts[h],
                               preferred_element_type=F32) + bias

    def consume(c, s_ref, carry):
        m_new = [jnp.maximum(carry[h][0], jnp.max(_fold_rows(s_ref[h], jnp.maximum), axis=0, keepdims=True))
                 for h in range(hpg)]
        out = []
        for h in range(hpg):
            m, l, acc = carry[h]
            alpha = jnp.exp(m - m_new[h])
            p = jnp.exp(s_ref[h] - m_new[h])
            l = alpha * l + _fold_rows(p, jnp.add)
            acc = alpha * acc + jnp.dot(vt_ref[c, h * HEAD:(h + 1) * HEAD, :], p.astype(BF16),
                                        preferred_element_type=F32)
            out.append((m_new[h], l, acc))
        return tuple(out)

    put_scores(0, sa_ref)

    def pair(j, carry):
        put_scores(2 * j + 1, sb_ref)
        carry = consume(2 * j, sa_ref, carry)
        put_scores(jnp.minimum(2 * j + 2, n_total - 1), sa_ref)
        return consume(2 * j + 1, sb_ref, carry)

    init = tuple((jnp.full((1, nq), -jnp.inf, F32), jnp.zeros((8, nq), F32), jnp.zeros((HEAD, nq), F32))
                 for _ in range(hpg))
    res = lax.fori_loop(0, (n_ch + 1) // 2, pair, init)
    for h in range(hpg):
        _, l, acc = res[h]
        out = acc / jnp.sum(l, axis=0, keepdims=True)
        for a in range(qb):
            o_ref[a * LANE:(a + 1) * LANE, h * HEAD:(h + 1) * HEAD] = (
                out[:, a * LANE:(a + 1) * LANE].T.astype(o_ref.dtype))


def _attend_prompt(qa, ka, vat, bias, dm):
    hpg = 4 if dm.heads % 4 == 0 else 1
    qb = QUERY_BLOCKS
    nq = qb * LANE
    ng = dm.heads // hpg
    w = hpg * HEAD
    once = pl.Buffered(1)
    return pl.pallas_call(
        functools.partial(_attn_kernel, hpg=hpg, qb=qb, ch=dm.ch, cb=dm.cb, t_tiles=dm.t_pad // LANE),
        grid=(ng, dm.tq_pad // nq),
        in_specs=[pl.BlockSpec((nq, w), lambda g, i: (i, g)),
                  pl.BlockSpec((dm.tk_pad, w), lambda g, i: (0, g), pipeline_mode=once),
                  pl.BlockSpec((dm.tk_pad // dm.ch, w, dm.ch), lambda g, i: (0, g, 0), pipeline_mode=once),
                  pl.BlockSpec((qb, dm.tk_pad, LANE), lambda g, i: (i, 0, 0))],
        out_specs=pl.BlockSpec((nq, w), lambda g, i: (i, g)),
        out_shape=jax.ShapeDtypeStruct((dm.tq_pad, dm.d_r), BF16),
        scratch_shapes=[pltpu.VMEM((hpg, dm.ch, nq), F32), pltpu.VMEM((hpg, dm.ch, nq), F32)],
        compiler_params=_params("parallel", "arbitrary"),
        name="attend_prompt",
    )(qa, ka, vat, bias)


def _sample_index_kernel(pt_ref, *refs, n_pages, page, ih, ds, n_sel, scale):
    del pt_ref
    cik_refs = refs[:n_pages]
    qi_ref, kin_ref, w_ref, bias_ref, key_ref, kall_ref = refs[n_pages:]
    nq = ds * ih
    n_past = n_pages * page
    q2 = jnp.concatenate([qi_ref[:, h * HEAD:(h + 1) * HEAD] for h in range(ih)], axis=0).astype(BF16)
    w = w_ref[...] * scale
    sel = (lax.broadcasted_iota(I32, (nq, LANE), 0) % ds
           == lax.broadcasted_iota(I32, (nq, LANE), 1) % ds).astype(BF16)
    for p in range(n_pages):
        kall_ref[p * page:(p + 1) * page, :] = cik_refs[p][...].astype(BF16)
    kall_ref[n_past:, :] = jnp.concatenate([kin_ref[...], jnp.zeros((page - ds, HEAD), F32)],
                                           axis=0).astype(BF16)
    s = lax.dot_general(kall_ref[...], q2, _NT, preferred_element_type=F32)
    keys = _to_key(_dot2_rhs(jnp.maximum(s, 0.0) * w, sel))
    key_ref[:n_past, :] = keys[:n_past]
    visible = (lax.broadcasted_iota(I32, (page, LANE), 0) <= lax.broadcasted_iota(I32, (page, LANE), 1) % ds)
    key_ref[n_past:, :] = jnp.where(visible, keys[n_past:], KEY_NEG_INF)

    def count_ge(cand):
        total = _count_ge(key_ref[0:page, :], cand)
        for r0 in range(page, n_past + page, page):
            total = total + _count_ge(key_ref[r0:r0 + page, :], cand)
        return total

    thr = _kth_largest_replicated(count_ge, n_sel, ds)
    bias_ref[...] = _select_bias(key_ref[...], thr).astype(bias_ref.dtype)


def _dot2_rhs(x, b):
    hi = x.astype(BF16)
    lo = (x - hi.astype(F32)).astype(BF16)
    return jnp.dot(hi, b, preferred_element_type=F32) + jnp.dot(lo, b, preferred_element_type=F32)


def _index_select_sample(page_table, cik, qi, ki, w_s, dm):
    assert dm.page == LANE and dm.heads * dm.ds == LANE and dm.t % dm.ds == 0
    nq = dm.ds * dm.ih
    n_keys = (dm.n_pages + 1) * dm.page
    assert n_keys >= dm.nsel_s
    r0 = dm.t // dm.ds
    page_spec = lambda p: pl.BlockSpec((None, dm.page, HEAD), lambda b, pt, p=p: (pt[b, p], 0, 0))
    grid_spec = pltpu.PrefetchScalarGridSpec(
        num_scalar_prefetch=1,
        grid=(dm.db,),
        in_specs=[page_spec(p) for p in range(dm.n_pages)] + [
            pl.BlockSpec((dm.ds, dm.ih * HEAD), lambda b, pt: (r0 + b, 0)),
            pl.BlockSpec((dm.ds, HEAD), lambda b, pt: (r0 + b, 0)),
            pl.BlockSpec((None, 1, nq), lambda b, pt: (b, 0, 0))],
        out_specs=pl.BlockSpec((None, n_keys, LANE), lambda b, pt: (b, 0, 0)),
        scratch_shapes=[pltpu.VMEM((n_keys, LANE), I32), pltpu.VMEM((n_keys, HEAD), BF16)],
    )
    return pl.pallas_call(
        functools.partial(_sample_index_kernel, n_pages=dm.n_pages, page=dm.page, ih=dm.ih, ds=dm.ds,
                          n_sel=dm.nsel_s, scale=dm.ih ** -0.5 * HEAD ** -0.5),
        grid_spec=grid_spec,
        out_shape=jax.ShapeDtypeStruct((dm.db, n_keys, LANE), BF16),
        compiler_params=_params("parallel"),
        name="index_select_sample",
    )(page_table, *([cik] * dm.n_pages), qi, ki, w_s)


def _sample_attn_kernel(pt_ref, *refs, pps, page, nh, ds):
    del pt_ref
    k_refs, v_refs = refs[:pps], refs[pps:2 * pps]
    q_ref, kn_ref, vn_ref, bias_ref, o_ref, qt_ref, onehot_ref, headmask_ref, m_ref, l_ref, acc_ref = refs[2 * pps:]
    b, g = pl.program_id(0), pl.program_id(1)
    rows = page * nh

    @pl.when((b == 0) & (g == 0))
    def _():
        r = lax.broadcasted_iota(I32, (rows, LANE), 0)
        c = lax.broadcasted_iota(I32, (rows, LANE), 1)
        onehot_ref[...] = (r // nh == c).astype(BF16)
        headmask_ref[...] = jnp.where(r % nh == c // ds, 0.0, MASKED)

    @pl.when(g == 0)
    def _():
        q8 = q_ref[...]
        qstack = jnp.concatenate([q8[:, h * HEAD:(h + 1) * HEAD] for h in range(nh)], axis=0)
        qt_ref[...] = qstack.T.astype(BF16)
        m_ref[...] = jnp.full_like(m_ref, -jnp.inf)
        l_ref[...] = jnp.zeros_like(l_ref)
        acc_ref[...] = jnp.zeros_like(acc_ref)

    def attend(k2, v2, onehot, headmask, bias_tile):
        rhs = jnp.concatenate([qt_ref[...], bias_tile], axis=0)
        n = k2.shape[0]
        step = min(n, 512)
        s = jnp.concatenate(
            [jnp.dot(jnp.concatenate([k2[r:r + step].astype(BF16), onehot[r:r + step]], axis=1), rhs,
                     preferred_element_type=F32) + headmask[r:r + step] for r in range(0, n, step)], axis=0)
        m_old = m_ref[...]
        m_new = jnp.maximum(m_old, jnp.max(_fold_rows(s, jnp.maximum), axis=0, keepdims=True))
        alpha = jnp.exp(m_old - m_new)
        p = jnp.exp(s - m_new)
        l_ref[...] = alpha * l_ref[...] + jnp.sum(_fold_rows(p, jnp.add), axis=0, keepdims=True)
        acc_ref[...] = alpha * acc_ref[...] + lax.dot_general(v2.astype(BF16), p.astype(BF16), _TN,
                                                              preferred_element_type=F32)
        m_ref[...] = m_new

    for p in range(pps):
        r0 = pl.multiple_of((g * pps + p) * page, page)
        attend(k_refs[p][...], v_refs[p][...], onehot_ref[...], headmask_ref[...], bias_ref[pl.ds(r0, page), :])

    @pl.when(g == pl.num_programs(1) - 1)
    def _():
        n_past = bias_ref.shape[0] - page
        by_head = lambda ref: jnp.concatenate([ref[:, h * HEAD:(h + 1) * HEAD] for h in range(nh)], axis=0)
        r = lax.broadcasted_iota(I32, (nh * ds, LANE), 0)
        c = lax.broadcasted_iota(I32, (nh * ds, LANE), 1)
        attend(by_head(kn_ref), by_head(vn_ref), (r % ds == c).astype(BF16),
               jnp.where(r // ds == c // ds, 0.0, MASKED), bias_ref[n_past:, :])
        out = (acc_ref[...] / l_ref[...]).T
        for h in range(nh):
            o_ref[:, h * HEAD:(h + 1) * HEAD] = out[h * ds:(h + 1) * ds, :]


def _attend_sample(page_table, ck, cv, qa, k_new, v_new, bias, dm):
    assert dm.page == LANE and dm.heads * dm.ds == LANE
    pps = max(p for p in (8, 4, 2, 1) if dm.n_pages % p == 0)
    w = dm.heads * HEAD
    rows = dm.page * dm.heads
    n_keys = (dm.n_pages + 1) * dm.page
    r0 = dm.t // dm.ds
    page_spec = lambda p: pl.BlockSpec((None, rows, HEAD), lambda b, g, pt, p=p: (pt[b, g * pps + p], 0, 0))
    grid_spec = pltpu.PrefetchScalarGridSpec(
        num_scalar_prefetch=1,
        grid=(dm.db, dm.n_pages // pps),
        in_specs=[page_spec(p) for p in range(pps)] * 2 + [
            pl.BlockSpec((dm.ds, w), lambda b, g, pt: (r0 + b, 0)),
            pl.BlockSpec((dm.ds, w), lambda b, g, pt: (b, 0)),
            pl.BlockSpec((dm.ds, w), lambda b, g, pt: (b, 0)),
            pl.BlockSpec((None, n_keys, LANE), lambda b, g, pt: (b, 0, 0))],
        out_specs=pl.BlockSpec((dm.ds, w), lambda b, g, pt: (b, 0)),
        scratch_shapes=[pltpu.VMEM((HEAD, LANE), BF16), pltpu.VMEM((rows, LANE), BF16),
                        pltpu.VMEM((rows, LANE), F32), pltpu.VMEM((1, LANE), F32), pltpu.VMEM((1, LANE), F32),
                        pltpu.VMEM((HEAD, LANE), F32)],
    )
    n_pool = ck.shape[0]
    return pl.pallas_call(
        functools.partial(_sample_attn_kernel, pps=pps, page=dm.page, nh=dm.heads, ds=dm.ds),
        grid_spec=grid_spec,
        out_shape=jax.ShapeDtypeStruct((dm.db * dm.ds, w), F32),
        compiler_params=_params("arbitrary", "arbitrary"),
        name="attend_sample",
    )(page_table, *([ck.reshape(n_pool, rows, HEAD)] * pps), *([cv.reshape(n_pool, rows, HEAD)] * pps),
      qa, k_new, v_new, bias)


def kernel(x_prompt, x_sample, cache_k, cache_v, cache_idx_k, state_hgrn, page_table, meta_tokens, lb_logits,
           norm_mix_g, w_in, q_norm_g, w_uq, w_uq_idx, idx_k_norm_g, idx_k_norm_b, hgrn_norm_g, w_out,
           norm_ffn_g, w_gate, w_up, w_down, final_norm_g):
    assert x_prompt.shape[0] == 1 and w_in.shape[0] == 1, "single prompt sequence, single layer"
    dm = _dims(x_prompt, x_sample, cache_k, page_table)
    d, t, r, rp = dm.d, dm.t, dm.r, dm.rp
    ns = dm.db * dm.ds
    off_cq = 4 * dm.d_r
    off_k = off_cq + dm.q_rank
    off_v = off_k + dm.d_r
    off_ik = off_v + dm.d_r

    h0 = jnp.concatenate([meta_tokens.astype(F32), x_prompt[0], x_sample.reshape(ns, d),
                          jnp.zeros((rp - r, d), F32)], axis=0)
    hn = _rmsnorm(h0, norm_mix_g[0], width=d, col_block=0, out_dtype=BF16)
    w_in_t = w_in[0].T
    z = _matmul_nt(hn, w_in_t, dm, out_dtype=F32, n_cols=off_ik, name="in_proj")
    w_tail = jnp.pad(w_in_t[off_ik:], ((0, 2 * HEAD - (dm.d_in - off_ik)), (0, 0)))
    z_tail = _matmul_nt(hn, w_tail, dm, out_dtype=F32, name="in_proj_idx")

    k_s = z[t:r, off_k:off_k + dm.d_r]
    v_s = z[t:r, off_v:off_v + dm.d_r]

    o_r_p, s_p = _hgrn_prompt(z, lb_logits, hgrn_norm_g[0], dm)
    o_r_s, s_s = _hgrn_sample(z, lb_logits, hgrn_norm_g[0], state_hgrn[0], dm)

    cqn = _rmsnorm(z, q_norm_g[0], width=dm.q_rank, col_block=off_cq // dm.q_rank, out_dtype=BF16)
    qa = _matmul(cqn, w_uq[0], dm, out_dtype=F32, scale=HEAD ** -0.5, name="q_up")
    qi = _matmul(cqn, w_uq_idx[0], dm, out_dtype=F32, name="q_idx_up")
    ki, ki_b, wt = _idx_prep(z_tail, idx_k_norm_g[0], idx_k_norm_b[0], dm, 0)

    ka, vat = _attn_operands(z, off_k, off_v, dm)
    bias_p = _index_select_prompt(qi, ki_b, wt, dm)
    o_a_p = _attend_prompt(qa, ka, vat, bias_p, dm)

    w_s = z_tail[t:r, HEAD:HEAD + dm.ih].reshape(dm.db, dm.ds, dm.ih).swapaxes(1, 2).reshape(dm.db, 1, dm.ds * dm.ih)
    bias_s = _index_select_sample(page_table, cache_idx_k[0], qi, ki, w_s, dm)
    o_a_s = _attend_sample(page_table, cache_k[0], cache_v[0], qa, k_s, v_s, bias_s, dm)

    o_mix = jnp.concatenate([
        jnp.concatenate([o_r_p[:t], o_a_p[:t]], axis=1),
        jnp.concatenate([o_r_s, o_a_s], axis=1).astype(BF16),
        jnp.zeros((rp - r, 2 * dm.d_r), BF16)], axis=0)
    h1 = _matmul(o_mix, w_out[0], dm, out_dtype=F32, residual=h0, name="out_proj")

    hf = _rmsnorm(h1, norm_ffn_g[0], width=d, col_block=0, out_dtype=BF16)
    act = _ffn_up(hf, w_gate[0], w_up[0], dm)
    h2 = _matmul(act, w_down[0].astype(BF16), dm, out_dtype=F32, residual=h1, tn_max=256, name="ffn_down")
    y_prompt = _rmsnorm_rows(h2, final_norm_g, N_META, t - N_META)
    y_sample = _rmsnorm_rows(h2, final_norm_g, t, ns)

    sample_heads = lambda a: a.reshape(1, dm.db, dm.ds, dm.heads, HEAD)
    return (y_prompt[None],
            y_sample.reshape(dm.db, dm.ds, d),
            _prompt_heads(z, off_k, dm)[None, None], _prompt_heads(z, off_v, dm)[None, None],
            ki[:t].reshape(1, 1, t, HEAD), s_p[None, None],
            sample_heads(k_s), sample_heads(v_s), ki[t:r].reshape(1, dm.db, dm.ds, HEAD), s_s[None])
```

```python
import functools
from typing import NamedTuple

import jax
import jax.numpy as jnp
from jax import lax
from jax.experimental import pallas as pl
from jax.experimental.pallas import tpu as pltpu

F32 = jnp.float32
BF16 = jnp.bfloat16
I32 = jnp.int32

EPS = 1e-6
N_META = 16
HEAD = 128
TOPK_MAX = 256
LANE = 128
VMEM_LIMIT_BYTES = 56 * 2**20
QUERY_BLOCKS = 2
MASKED = -1e30
INT_MIN = -2**31
KEY_NEG_INF = -2139095041

_NT = (((1,), (1,)), ((), ()))
_TN = (((0,), (0,)), ((), ()))


class Dims(NamedTuple):
    d: int
    t: int
    t_pad: int
    tq_pad: int
    tk_pad: int
    db: int
    ds: int
    r: int
    rp: int
    tm: int
    heads: int
    ih: int
    d_r: int
    q_rank: int
    d_in: int
    d_ff: int
    n_pages: int
    page: int
    nsel_p: int
    nsel_s: int
    cb: int
    ch: int


def _round_up(x, m):
    return -(-x // m) * m


def _dims(x_prompt, x_sample, cache_k, page_table):
    d = x_prompt.shape[-1]
    seq = x_prompt.shape[1]
    db, ds = x_sample.shape[:2]
    t = seq + N_META
    t_pad = _round_up(t, LANE)
    nb = t_pad // LANE
    cb = 5 if nb % 5 == 0 else 1
    tq_pad = _round_up(t, QUERY_BLOCKS * LANE)
    tk_pad = _round_up(t_pad, 2 * cb * LANE)
    r = t + db * ds
    rp = max(_round_up(r, LANE), tq_pad, tk_pad)
    n_tiles = max(1, -(-rp // 1200))
    while (rp // 16) % n_tiles:
        n_tiles += 1
    heads = d // 256
    d_r = heads * HEAD
    q_rank = d // 4
    ih = d // 128
    d_in = 6 * d_r + q_rank + HEAD + ih
    d_ff = -(-8 * d // (3 * 256)) * 256
    n_pages, page = page_table.shape[1], cache_k.shape[2]
    past = n_pages * page
    return Dims(d=d, t=t, t_pad=t_pad, tq_pad=tq_pad, tk_pad=tk_pad, db=db, ds=ds, r=r, rp=rp, tm=rp // n_tiles, heads=heads, ih=ih,
                d_r=d_r, q_rank=q_rank, d_in=d_in, d_ff=d_ff,
                n_pages=n_pages, page=page,
                nsel_p=min(TOPK_MAX, seq // 4), nsel_s=min(TOPK_MAX, (past + ds) // 4),
                cb=cb, ch=cb * LANE)


def _tile(n, pref):
    best = LANE
    for c in range(LANE, min(n, pref) + 1, LANE):
        if n % c == 0:
            best = c
    return best


def _params(*sem):
    return pltpu.CompilerParams(dimension_semantics=sem, vmem_limit_bytes=VMEM_LIMIT_BYTES)


def _sigmoid(x):
    return 1.0 / (1.0 + jnp.exp(-x))


def _silu(x):
    return x * _sigmoid(x)


def _rmsnorm_kernel(x_ref, g_ref, o_ref):
    x = x_ref[...]
    ms = jnp.mean(x * x, axis=-1, keepdims=True)
    o_ref[...] = (x * lax.rsqrt(ms + EPS) * g_ref[...]).astype(o_ref.dtype)


def _rmsnorm(x, g, *, width, col_block, out_dtype):
    rows = x.shape[0]
    tr = LANE
    return pl.pallas_call(
        _rmsnorm_kernel,
        grid=(rows // tr,),
        in_specs=[pl.BlockSpec((tr, width), lambda i: (i, col_block)),
                  pl.BlockSpec((1, width), lambda i: (0, 0))],
        out_specs=pl.BlockSpec((tr, width), lambda i: (i, 0)),
        out_shape=jax.ShapeDtypeStruct((rows, width), out_dtype),
        compiler_params=_params("parallel"),
        name="rmsnorm",
    )(x, g.reshape(1, width).astype(F32))


def _rmsnorm_rows(x, g, row0, n_rows):
    width = x.shape[1]
    tr = max(c for c in range(8, LANE + 1, 8) if n_rows % c == 0)
    assert row0 % 8 == 0
    return pl.pallas_call(
        _rmsnorm_kernel,
        grid=(n_rows // tr,),
        in_specs=[pl.BlockSpec((pl.Element(tr), pl.Element(width)), lambda i: (pl.multiple_of(row0 + i * tr, 8), 0)),
                  pl.BlockSpec((1, width), lambda i: (0, 0))],
        out_specs=pl.BlockSpec((tr, width), lambda i: (i, 0)),
        out_shape=jax.ShapeDtypeStruct((n_rows, width), F32),
        compiler_params=_params("parallel"),
        name="rmsnorm_rows",
    )(x, g.reshape(1, width).astype(F32))


def _idx_prep_kernel(ik_ref, iw_ref, g_ref, b_ref, ki_ref, kib_ref, wt_ref, *, ih, scale):
    x = ik_ref[...]
    mu = jnp.mean(x, axis=-1, keepdims=True)
    xc = x - mu
    var = jnp.mean(xc * xc, axis=-1, keepdims=True)
    ki = xc * lax.rsqrt(var + EPS) * g_ref[...] + b_ref[...]
    ki_ref[...] = ki
    kib_ref[...] = ki.astype(BF16)
    wt_ref[...] = (iw_ref[...] * scale).T[:ih, :]


def _idx_prep(z, g, b, dm, off_ik):
    tr = LANE
    cb = off_ik // HEAD
    return pl.pallas_call(
        functools.partial(_idx_prep_kernel, ih=dm.ih, scale=dm.ih ** -0.5 * HEAD ** -0.5),
        grid=(dm.rp // tr,),
        in_specs=[pl.BlockSpec((tr, HEAD), lambda i: (i, cb)),
                  pl.BlockSpec((tr, HEAD), lambda i: (i, cb + 1)),
                  pl.BlockSpec((1, HEAD), lambda i: (0, 0)),
                  pl.BlockSpec((1, HEAD), lambda i: (0, 0))],
        out_specs=[pl.BlockSpec((tr, HEAD), lambda i: (i, 0)),
                   pl.BlockSpec((tr, HEAD), lambda i: (i, 0)),
                   pl.BlockSpec((dm.ih, tr), lambda i: (0, i))],
        out_shape=[jax.ShapeDtypeStruct((dm.rp, HEAD), F32),
                   jax.ShapeDtypeStruct((dm.rp, HEAD), BF16),
                   jax.ShapeDtypeStruct((dm.ih, dm.rp), F32)],
        compiler_params=_params("parallel"),
        name="idx_prep",
    )(z, z, g.reshape(1, HEAD), b.reshape(1, HEAD))


def _heads_out_kernel(x0_ref, x1_ref, o_ref, *, nh):
    half = nh // 2
    for h in range(nh):
        src = x0_ref if h < half else x1_ref
        o_ref[:, h, :] = src[:, (h % half) * HEAD:(h % half + 1) * HEAD]


def _prompt_heads(z, off, dm):
    wblk = dm.d_r // 2
    tr = max(c for c in range(8, 513, 8) if dm.t % c == 0)
    cb = off // wblk
    assert off % wblk == 0 and dm.heads % 2 == 0
    return pl.pallas_call(
        functools.partial(_heads_out_kernel, nh=dm.heads),
        grid=(dm.t // tr,),
        in_specs=[pl.BlockSpec((tr, wblk), lambda i: (i, cb)),
                  pl.BlockSpec((tr, wblk), lambda i: (i, cb + 1))],
        out_specs=pl.BlockSpec((tr, dm.heads, HEAD), lambda i: (i, 0, 0)),
        out_shape=jax.ShapeDtypeStruct((dm.t, dm.heads, HEAD), F32),
        compiler_params=_params("parallel"),
        name="prompt_heads",
    )(z, z)


def _attn_operands_kernel(k0_ref, k1_ref, v0_ref, v1_ref, ka_ref, vat_ref):
    w = k0_ref.shape[1]
    ka_ref[:, :w] = k0_ref[...].astype(BF16)
    ka_ref[:, w:] = k1_ref[...].astype(BF16)
    for c0 in range(0, w, LANE):
        vat_ref[c0:c0 + LANE, :] = v0_ref[:, c0:c0 + LANE].T.astype(BF16)
        vat_ref[w + c0:w + c0 + LANE, :] = v1_ref[:, c0:c0 + LANE].T.astype(BF16)


def _attn_operands(z, off_k, off_v, dm):
    wblk = dm.d_r // 2
    ck, cv = off_k // wblk, off_v // wblk
    assert off_k % wblk == 0 and off_v % wblk == 0
    blk = lambda c: pl.BlockSpec((LANE, wblk), lambda i, c=c: (i, c))
    return pl.pallas_call(
        _attn_operands_kernel,
        grid=(dm.tk_pad // LANE,),
        in_specs=[blk(ck), blk(ck + 1), blk(cv), blk(cv + 1)],
        out_specs=[pl.BlockSpec((LANE, dm.d_r), lambda i: (i, 0)),
                   pl.BlockSpec((None, dm.d_r, LANE), lambda i: (i // dm.cb, 0, i % dm.cb))],
        out_shape=[jax.ShapeDtypeStruct((dm.tk_pad, dm.d_r), BF16),
                   jax.ShapeDtypeStruct((dm.tk_pad // dm.ch, dm.d_r, dm.ch), BF16)],
        compiler_params=_params("parallel"),
        name="attn_operands",
    )(z, z, z, z)


def _weight_tile(w_ref):
    w = w_ref[...]
    return w if w.dtype == BF16 else w.astype(BF16)


def _mm_kernel(a_ref, b_ref, o_ref, *, scale):
    acc = jnp.dot(a_ref[...], _weight_tile(b_ref), preferred_element_type=F32)
    if scale != 1.0:
        acc = acc * scale
    o_ref[...] = acc.astype(o_ref.dtype)


def _mm_res_kernel(a_ref, b_ref, r_ref, o_ref):
    o_ref[...] = r_ref[...] + jnp.dot(a_ref[...], _weight_tile(b_ref), preferred_element_type=F32)


def _matmul(a, b, dm, *, out_dtype, name, scale=1.0, residual=None, n_cols=None, tn_max=512):
    m, k = a.shape
    n = b.shape[1] if n_cols is None else n_cols
    tm, tn = dm.tm, _tile(n, tn_max)
    a_mode = {"pipeline_mode": pl.Buffered(1)} if tm * k * a.dtype.itemsize > 16 * 2**20 else {}
    in_specs = [pl.BlockSpec((tm, k), lambda i, j: (i, 0), **a_mode),
                pl.BlockSpec((k, tn), lambda i, j: (0, j))]
    args = [a, b]
    if residual is None:
        body = functools.partial(_mm_kernel, scale=scale)
    else:
        body = _mm_res_kernel
        in_specs.append(pl.BlockSpec((tm, tn), lambda i, j: (i, j)))
        args.append(residual)
    return pl.pallas_call(
        body,
        grid=(m // tm, n // tn),
        in_specs=in_specs,
        out_specs=pl.BlockSpec((tm, tn), lambda i, j: (i, j)),
        out_shape=jax.ShapeDtypeStruct((m, n), out_dtype),
        compiler_params=_params("parallel", "parallel"),
        name=name,
    )(*args)


def _out_proj_kernel(o_r_ref, o_a_ref, tail_ref, w_ref, r_ref, o_ref, *, n_main):
    i = pl.program_id(0)
    half = o_r_ref.shape[1]

    @pl.when(i < n_main)
    def _():
        w = _weight_tile(w_ref)
        o_ref[...] = (r_ref[...] + jnp.dot(o_r_ref[...], w[:half], preferred_element_type=F32)
                      + jnp.dot(o_a_ref[...], w[half:], preferred_element_type=F32))

    @pl.when(i >= n_main)
    def _():
        o_ref[...] = r_ref[...] + jnp.dot(tail_ref[...], _weight_tile(w_ref), preferred_element_type=F32)


def _out_proj(o_r_p, o_a_p, tail, w, residual, dm, n_main):
    m, n = residual.shape
    half = o_r_p.shape[1]
    tm, tn = dm.tm, _tile(n, 512)
    once = pl.Buffered(1)
    main_idx = lambda i, j: (jnp.minimum(i, n_main - 1), 0)
    return pl.pallas_call(
        functools.partial(_out_proj_kernel, n_main=n_main),
        grid=(m // tm, n // tn),
        in_specs=[pl.BlockSpec((tm, half), main_idx, pipeline_mode=once),
                  pl.BlockSpec((tm, half), main_idx, pipeline_mode=once),
                  pl.BlockSpec((tm, 2 * half), lambda i, j: (jnp.maximum(i - n_main, 0), 0), pipeline_mode=once),
                  pl.BlockSpec((2 * half, tn), lambda i, j: (0, j)),
                  pl.BlockSpec((tm, tn), lambda i, j: (i, j))],
        out_specs=pl.BlockSpec((tm, tn), lambda i, j: (i, j)),
        out_shape=jax.ShapeDtypeStruct((m, n), F32),
        compiler_params=_params("parallel", "parallel"),
        name="out_proj",
    )(o_r_p, o_a_p, tail, w, residual)


def _mm_nt_kernel(a_ref, bt_ref, o_ref):
    o_ref[...] = lax.dot_general(a_ref[...], _weight_tile(bt_ref), _NT,
                                 preferred_element_type=F32).astype(o_ref.dtype)


def _matmul_nt(a, bt, dm, *, out_dtype, name, n_cols=None):
    m, k = a.shape
    n = bt.shape[0] if n_cols is None else n_cols
    tm, tn = dm.tm, _tile(n, 512)
    return pl.pallas_call(
        _mm_nt_kernel,
        grid=(m // tm, n // tn),
        in_specs=[pl.BlockSpec((tm, k), lambda i, j: (i, 0)),
                  pl.BlockSpec((tn, k), lambda i, j: (j, 0))],
        out_specs=pl.BlockSpec((tm, tn), lambda i, j: (i, j)),
        out_shape=jax.ShapeDtypeStruct((m, n), out_dtype),
        compiler_params=_params("parallel", "parallel"),
        name=name,
    )(a, bt)


def _ffn_up_kernel(x_ref, wg_ref, wu_ref, o_ref):
    x = x_ref[...]
    g = jnp.dot(x, _weight_tile(wg_ref), preferred_element_type=F32)
    u = jnp.dot(x, _weight_tile(wu_ref), preferred_element_type=F32)
    o_ref[...] = (_silu(g) * u).astype(o_ref.dtype)


def _ffn_up(x, wg, wu, dm):
    m, k = x.shape
    n = wg.shape[1]
    tm, tn = dm.tm, _tile(n, 256)
    return pl.pallas_call(
        _ffn_up_kernel,
        grid=(m // tm, n // tn),
        in_specs=[pl.BlockSpec((tm, k), lambda i, j: (i, 0)),
                  pl.BlockSpec((k, tn), lambda i, j: (0, j)),
                  pl.BlockSpec((k, tn), lambda i, j: (0, j))],
        out_specs=pl.BlockSpec((tm, tn), lambda i, j: (i, j)),
        out_shape=jax.ShapeDtypeStruct((m, n), BF16),
        compiler_params=_params("parallel", "parallel"),
        name="ffn_up",
    )(x, wg, wu)


def _split3(x):
    hi = x.astype(BF16)
    r = x - hi.astype(F32)
    mid = r.astype(BF16)
    lo = (r - mid.astype(F32)).astype(BF16)
    return hi, mid, lo


def _dot3(a, x):
    hi, mid, lo = _split3(x)
    return (jnp.dot(a, hi, preferred_element_type=F32) + jnp.dot(a, mid, preferred_element_type=F32)
            + jnp.dot(a, lo, preferred_element_type=F32))


def _lower_bound(lbl):
    e = jnp.exp(lbl - jnp.max(lbl, axis=0, keepdims=True))
    return e[0:1, :] / jnp.sum(e, axis=0, keepdims=True)


def _hgrn_gates(q_r, f_r, lb):
    key = (1.0 - lb) * _sigmoid(-f_r)
    return _silu(q_r), key, jnp.log(1.0 - key)


def _diag_block(q, key, g, v, ones):
    n = q.shape[0]
    rows = lax.broadcasted_iota(I32, (n, HEAD), 0)
    parts = []
    for s in range(n):
        lo = s // 8 * 8
        decay = jnp.exp(g[lo:] - g[s:s + 1, :])
        live = jnp.where(rows[lo:] >= s, q[lo:] * (key[s:s + 1, :] * decay), 0.0)
        parts.append(jnp.concatenate([jnp.zeros((lo, HEAD), F32), live], axis=0) if lo else live)
    p = jnp.concatenate(parts, axis=0).astype(BF16)
    a = jnp.dot(p, ones, preferred_element_type=F32)
    o_groups = [None] * (n // 8)
    for s in range(n):
        for gi in range(s // 8, n // 8):
            term = a[s * n + gi * 8:s * n + gi * 8 + 8] * v[s:s + 1, :]
            o_groups[gi] = term if o_groups[gi] is None else o_groups[gi] + term
    return jnp.concatenate(o_groups, axis=0) if len(o_groups) > 1 else o_groups[0]


def _hgrn_levels(q, key, g):
    c = q.shape[0]
    row = lax.broadcasted_iota(I32, (c, HEAD), 0)
    t_i = lax.broadcasted_iota(I32, (c, c), 0)
    s_i = lax.broadcasted_iota(I32, (c, c), 1)
    total = None
    b = c // 2
    while b >= 16:
        gref = g[b - 1:b, :]
        for m in range(2 * b, c, 2 * b):
            gref = jnp.where(row >= m, g[m + b - 1:m + b, :], gref)
        lower = (row & b) != 0
        qs = jnp.where(lower, q * jnp.exp(g - gref), 0.0)
        ks = jnp.where(lower, 0.0, key * jnp.exp(gref - g))
        a = lax.dot_general(qs.astype(BF16), ks.astype(BF16), _NT, preferred_element_type=F32)
        if 2 * b < c:
            a = jnp.where((t_i // (2 * b)) == (s_i // (2 * b)), a, 0.0)
        total = a if total is None else total + a
        b //= 2
    return total


def _scan_rows8(x):
    row = lax.broadcasted_iota(I32, x.shape, 0)
    for sh in (1, 2, 4):
        x = x + jnp.where(row >= sh, pltpu.roll(x, sh, 0), 0.0)
    return x


def _hgrn_out(o, g_norm, og):
    ms = jnp.mean(o * o, axis=-1, keepdims=True)
    return o * lax.rsqrt(ms + EPS) * g_norm * _silu(og)


def _hgrn_prompt_kernel(q_ref, f_ref, i_ref, og_ref, lbl_ref, gn_ref, o_ref, s_ref, st_ref, *, t, hpg):
    c = pl.program_id(1)

    @pl.when(c == 0)
    def _():
        st_ref[...] = jnp.zeros_like(st_ref)

    valid = (c * LANE + lax.broadcasted_iota(I32, (LANE, HEAD), 0)) < t
    tri = (lax.broadcasted_iota(I32, (LANE, LANE), 0) >= lax.broadcasted_iota(I32, (LANE, LANE), 1)).astype(BF16)
    ones = jnp.ones((HEAD, HEAD), BF16)
    for h in range(hpg):
        sl = slice(h * HEAD, (h + 1) * HEAD)
        q, key, lf = _hgrn_gates(q_ref[:, sl], f_ref[:, sl], _lower_bound(lbl_ref[:, sl]))
        v = i_ref[:, sl]
        key = jnp.where(valid, key, 0.0)
        lf = jnp.where(valid, lf, 0.0)
        g = _dot3(tri, lf)
        g_last = g[LANE - 1:LANE, :]
        st = st_ref[h]
        vb = v.astype(BF16)
        o = lax.dot_general((q * jnp.exp(g)).astype(BF16), st.astype(BF16), _NT, preferred_element_type=F32)
        o = o + jnp.dot(_hgrn_levels(q, key, g).astype(BF16), vb, preferred_element_type=F32)
        o = o + jnp.concatenate(
            [_diag_block(q[m:m + 16], key[m:m + 16], g[m:m + 16], v[m:m + 16], ones) for m in range(0, LANE, 16)],
            axis=0)
        kd = (key * jnp.exp(g_last - g)).astype(BF16)
        st_new = jnp.exp(g_last) * st + lax.dot_general(vb, kd, _TN, preferred_element_type=F32)
        st_ref[h] = st_new
        o_ref[:, sl] = _hgrn_out(o, gn_ref[...], og_ref[:, sl]).astype(o_ref.dtype)

    @pl.when(c == pl.num_programs(1) - 1)
    def _():
        for h in range(hpg):
            s_ref[h] = st_ref[h].T


def _hgrn_prompt(z, lb_logits, g_norm, dm):
    hpg = max(p for p in (4, 2, 1) if dm.heads % p == 0)
    ng, nc, w = dm.heads // hpg, dm.t_pad // LANE, hpg * HEAD
    blk = lambda off: pl.BlockSpec((LANE, w), lambda g, c, off=off: (c, off * ng + g))
    nl = lb_logits.shape[0]
    return pl.pallas_call(
        functools.partial(_hgrn_prompt_kernel, t=dm.t, hpg=hpg),
        grid=(ng, nc),
        in_specs=[blk(0), blk(1), blk(2), blk(3),
                  pl.BlockSpec((nl, w), lambda g, c: (0, g)),
                  pl.BlockSpec((1, HEAD), lambda g, c: (0, 0))],
        out_specs=[pl.BlockSpec((LANE, w), lambda g, c: (c, g)),
                   pl.BlockSpec((hpg, HEAD, HEAD), lambda g, c: (g, 0, 0))],
        out_shape=[jax.ShapeDtypeStruct((dm.t_pad, dm.d_r), BF16),
                   jax.ShapeDtypeStruct((dm.heads, HEAD, HEAD), F32)],
        scratch_shapes=[pltpu.VMEM((hpg, HEAD, HEAD), F32)],
        compiler_params=_params("parallel", "arbitrary"),
        name="hgrn_prompt",
    )(z, z, z, z, lb_logits, g_norm.reshape(1, HEAD))


def _hgrn_sample_kernel(q_ref, f_ref, i_ref, og_ref, lbl_ref, gn_ref, s0_ref, o_ref, s_ref, *, nh, ds):
    q, key, lf = _hgrn_gates(q_ref[...], f_ref[...], _lower_bound(lbl_ref[...]))
    v = i_ref[...]
    g = _scan_rows8(lf)
    g_last = g[ds - 1:ds, :]
    qe = q * jnp.exp(g)
    kd = key * jnp.exp(g_last - g)
    e_last = jnp.exp(g_last)
    ones = jnp.ones((HEAD, HEAD), BF16)
    zpad = jnp.zeros((LANE - ds, HEAD), F32)
    og = og_ref[...]
    gn = gn_ref[...]
    for h in range(nh):
        sl = slice(h * HEAD, (h + 1) * HEAD)
        st = s0_ref[h].T
        o = lax.dot_general(qe[:, sl].astype(BF16), st.astype(BF16), _NT, preferred_element_type=F32)
        o = o + _diag_block(q[:, sl], key[:, sl], g[:, sl], v[:, sl], ones)
        vp = jnp.concatenate([v[:, sl], zpad], axis=0).astype(BF16)
        kp = jnp.concatenate([kd[:, sl], zpad], axis=0).astype(BF16)
        st_new = e_last[:, sl] * st + lax.dot_general(vp, kp, _TN, preferred_element_type=F32)
        s_ref[h] = st_new.T
        o_ref[:, sl] = _hgrn_out(o, gn, og[:, sl])


def _hgrn_sample(z, lb_logits, g_norm, state, dm):
    assert dm.ds == 8 and dm.t % dm.ds == 0
    nh, r0 = dm.heads, dm.t // dm.ds
    blk = lambda off: pl.BlockSpec((dm.ds, dm.d_r), lambda b, off=off: (r0 + b, off))
    nl = lb_logits.shape[0]
    return pl.pallas_call(
        functools.partial(_hgrn_sample_kernel, nh=nh, ds=dm.ds),
        grid=(dm.db,),
        in_specs=[blk(0), blk(1), blk(2), blk(3),
                  pl.BlockSpec((nl, dm.d_r), lambda b: (0, 0)),
                  pl.BlockSpec((1, HEAD), lambda b: (0, 0)),
                  pl.BlockSpec((None, nh, HEAD, HEAD), lambda b: (b, 0, 0, 0))],
        out_specs=[pl.BlockSpec((dm.ds, dm.d_r), lambda b: (b, 0)),
                   pl.BlockSpec((None, nh, HEAD, HEAD), lambda b: (b, 0, 0, 0))],
        out_shape=[jax.ShapeDtypeStruct((dm.db * dm.ds, dm.d_r), F32),
                   jax.ShapeDtypeStruct((dm.db, nh, HEAD, HEAD), F32)],
        compiler_params=_params("parallel"),
        name="hgrn_sample",
    )(z, z, z, z, lb_logits, g_norm.reshape(1, HEAD), state)


def _to_key(x):
    b = lax.bitcast_convert_type(x, I32)
    return b ^ (lax.shift_right_arithmetic(b, 31) & 0x7FFFFFFF)


def _fold_rows(x, op):
    total = None
    for r0 in range(0, x.shape[0], 256):
        parts = [x[r:r + 8] for r in range(r0, min(r0 + 256, x.shape[0]), 8)]
        while len(parts) > 1:
            parts = [op(parts[j], parts[j + 1]) for j in range(0, len(parts) - 1, 2)] + parts[len(parts) & ~1:]
        total = parts[0] if total is None else op(total, parts[0])
    return total


def _count_ge(keys, cand):
    return _fold_rows((keys >= cand).astype(I32), jnp.add)


def _kth_largest(count_ge, n_sel):
    def body(it, tp):
        bit = jnp.left_shift(jnp.int32(1), 31 - it)
        cand = tp | bit
        cnt = jnp.sum(count_ge(cand ^ INT_MIN), axis=0, keepdims=True)
        return jnp.where(cnt >= n_sel, cand, tp)
    return lax.fori_loop(0, 32, body, jnp.zeros((1, LANE), I32)) ^ INT_MIN


def _kth_largest_replicated(count_ge, n_sel, period):
    copies = LANE // period
    bits = copies.bit_length() - 1
    assert copies == 1 << bits and 32 % bits == 0
    digit = lax.broadcasted_iota(I32, (8, LANE), 1) // period

    def body(it, tp):
        shift = 32 - bits * (it + 1)
        cand = tp | jnp.left_shift(digit, shift)
        cnt = jnp.sum(count_ge(cand[0:1] ^ INT_MIN), axis=0, keepdims=True)
        ok = jnp.broadcast_to((cnt >= n_sel).astype(I32), (8, LANE))
        passed = _fold_rows(jnp.concatenate([ok] + [pltpu.roll(ok, j * period, 1) for j in range(1, copies)],
                                            axis=0), jnp.add)
        return tp | jnp.left_shift(passed - 1, shift)

    tp = lax.fori_loop(0, 32 // bits, body, jnp.zeros((8, LANE), I32))
    return tp[0:1] ^ INT_MIN


def _select_bias(keys, thr):
    return jnp.where((keys >= thr) & (keys > KEY_NEG_INF), 0.0, MASKED)


def _index_kernel(qi_ref, ki_ref, wt_ref, bias_ref, key_ref, qt_ref, *, ih, ch, cb, n_sel):
    i = pl.program_id(0)
    n_tiles = jnp.minimum(i + 1, key_ref.shape[0] // LANE)
    n_ch = (n_tiles + cb - 1) // cb
    n_total = bias_ref.shape[0] // ch
    for hp in range(ih // 2):
        pair = [qi_ref[:, (2 * hp + e) * HEAD:(2 * hp + e + 1) * HEAD].T for e in (0, 1)]
        qt_ref[hp] = jnp.concatenate(pair, axis=1).astype(BF16)
    rows = 2 * LANE
    q_pos = i * LANE + lax.broadcasted_iota(I32, (rows, LANE), 1)
    k_off = lax.broadcasted_iota(I32, (rows, LANE), 0)
    n_pairs = (n_tiles + 1) // 2

    def score_tiles(c, carry):
        r0 = pl.multiple_of(jnp.minimum(c * rows, key_ref.shape[0] - rows), LANE)
        kc = ki_ref[pl.ds(r0, rows), :]
        acc = jnp.zeros((rows, LANE), F32)
        for hp in range(ih // 2):
            s = jnp.dot(kc, qt_ref[hp], preferred_element_type=F32)
            acc = acc + jnp.maximum(s[:, :LANE], 0.0) * wt_ref[pl.ds(2 * hp, 1), :]
            acc = acc + jnp.maximum(s[:, LANE:], 0.0) * wt_ref[pl.ds(2 * hp + 1, 1), :]
        key_ref[pl.ds(r0, rows), :] = jnp.where(r0 + k_off <= q_pos, _to_key(acc), KEY_NEG_INF)
        return carry

    lax.fori_loop(0, n_pairs, score_tiles, 0)

    def fill_tile(c, carry):
        key_ref[pl.ds(pl.multiple_of(c * LANE, LANE), LANE), :] = jnp.full((LANE, LANE), KEY_NEG_INF, I32)
        return carry

    lax.fori_loop(2 * n_pairs, n_ch * cb, fill_tile, 0)

    def count_ge(cand):
        def body(c, cnt):
            return cnt + _count_ge(key_ref[pl.ds(pl.multiple_of(c * ch, ch), ch), :], cand)
        return lax.fori_loop(0, n_ch, body, jnp.zeros((8, LANE), I32))

    thr = _kth_largest(count_ge, n_sel)

    def write_bias(c, carry):
        r0 = pl.multiple_of(c * ch, ch)
        bias_ref[pl.ds(r0, ch), :] = _select_bias(key_ref[pl.ds(r0, ch), :], thr).astype(bias_ref.dtype)
        return carry

    lax.fori_loop(0, n_ch, write_bias, 0)

    def write_masked(c, carry):
        bias_ref[pl.ds(pl.multiple_of(c * ch, ch), ch), :] = jnp.full((ch, LANE), MASKED, bias_ref.dtype)
        return carry

    lax.fori_loop(n_ch, n_total, write_masked, 0)


def _index_select_prompt(qi, ki_b, wt, dm):
    nb = dm.tq_pad // LANE
    assert dm.ch >= dm.nsel_p and dm.ih % 2 == 0
    return pl.pallas_call(
        functools.partial(_index_kernel, ih=dm.ih, ch=dm.ch, cb=dm.cb, n_sel=dm.nsel_p),
        grid=(nb,),
        in_specs=[pl.BlockSpec((LANE, dm.ih * HEAD), lambda i: (i, 0)),
                  pl.BlockSpec((dm.t_pad, HEAD), lambda i: (0, 0)),
                  pl.BlockSpec((dm.ih, LANE), lambda i: (0, i))],
        out_specs=pl.BlockSpec((None, dm.tk_pad, LANE), lambda i: (i, 0, 0)),
        out_shape=jax.ShapeDtypeStruct((nb, dm.tk_pad, LANE), BF16),
        scratch_shapes=[pltpu.VMEM((dm.t_pad, LANE), I32), pltpu.VMEM((dm.ih // 2, HEAD, 2 * LANE), BF16)],
        compiler_params=_params("parallel"),
        name="index_select_prompt",
    )(qi, ki_b, wt)


def _attn_kernel(q_ref, k_ref, vt_ref, bias_ref, o_ref, sa_ref, sb_ref, *, hpg, qb, ch, cb, t_tiles):
    i = pl.program_id(1)
    n_ch = (jnp.minimum(qb * (i + 1), t_tiles) + cb - 1) // cb
    n_total = k_ref.shape[0] // ch
    assert n_total % 2 == 0
    nq = qb * LANE
    qts = [jnp.concatenate([q_ref[a * LANE:(a + 1) * LANE, h * HEAD:(h + 1) * HEAD].T for a in range(qb)],
                           axis=1).astype(BF16) for h in range(hpg)]

    def put_scores(c, s_ref):
        r0 = pl.multiple_of(c * ch, ch)
        bias = jnp.concatenate([bias_ref[a, pl.ds(r0, ch), :] for a in range(qb)], axis=1).astype(F32)
        for h in range(hpg):
            s_ref[h] = jnp.dot(k_ref[pl.ds(r0, ch), h * HEAD:(h + 1) * HEAD], qts[h],
                               preferred_element_type=F32) + bias

    def consume(c, s_ref, carry):
        m_new = [jnp.maximum(carry[h][0], jnp.max(_fold_rows(s_ref[h], jnp.maximum), axis=0, keepdims=True))
                 for h in range(hpg)]
        out = []
        for h in range(hpg):
            m, l, acc = carry[h]
            alpha = jnp.exp(m - m_new[h])
            p = jnp.exp(s_ref[h] - m_new[h])
            l = alpha * l + _fold_rows(p, jnp.add)
            acc = alpha * acc + jnp.dot(vt_ref[c, h * HEAD:(h + 1) * HEAD, :], p.astype(BF16),
                                        preferred_element_type=F32)
            out.append((m_new[h], l, acc))
        return tuple(out)

    put_scores(0, sa_ref)

    def pair(j, carry):
        put_scores(2 * j + 1, sb_ref)
        carry = consume(2 * j, sa_ref, carry)
        put_scores(jnp.minimum(2 * j + 2, n_total - 1), sa_ref)
        return consume(2 * j + 1, sb_ref, carry)

    init = tuple((jnp.full((1, nq), -jnp.inf, F32), jnp.zeros((8, nq), F32), jnp.zeros((HEAD, nq), F32))
                 for _ in range(hpg))
    res = lax.fori_loop(0, (n_ch + 1) // 2, pair, init)
    for h in range(hpg):
        _, l, acc = res[h]
        out = acc / jnp.sum(l, axis=0, keepdims=True)
        for a in range(qb):
            o_ref[a * LANE:(a + 1) * LANE, h * HEAD:(h + 1) * HEAD] = (
                out[:, a * LANE:(a + 1) * LANE].T.astype(o_ref.dtype))


def _attend_prompt(qa, ka, vat, bias, dm):
    hpg = 4 if dm.heads % 4 == 0 else 1
    qb = QUERY_BLOCKS
    nq = qb * LANE
    ng = dm.heads // hpg
    w = hpg * HEAD
    once = pl.Buffered(1)
    return pl.pallas_call(
        functools.partial(_attn_kernel, hpg=hpg, qb=qb, ch=dm.ch, cb=dm.cb, t_tiles=dm.t_pad // LANE),
        grid=(ng, dm.tq_pad // nq),
        in_specs=[pl.BlockSpec((nq, w), lambda g, i: (i, g)),
                  pl.BlockSpec((dm.tk_pad, w), lambda g, i: (0, g), pipeline_mode=once),
                  pl.BlockSpec((dm.tk_pad // dm.ch, w, dm.ch), lambda g, i: (0, g, 0), pipeline_mode=once),
                  pl.BlockSpec((qb, dm.tk_pad, LANE), lambda g, i: (i, 0, 0))],
        out_specs=pl.BlockSpec((nq, w), lambda g, i: (i, g)),
        out_shape=jax.ShapeDtypeStruct((dm.tq_pad, dm.d_r), BF16),
        scratch_shapes=[pltpu.VMEM((hpg, dm.ch, nq), F32), pltpu.VMEM((hpg, dm.ch, nq), F32)],
        compiler_params=_params("parallel", "arbitrary"),
        name="attend_prompt",
    )(qa, ka, vat, bias)


def _sample_index_kernel(pt_ref, *refs, n_pages, page, ih, ds, n_sel, scale):
    del pt_ref
    cik_refs = refs[:n_pages]
    qi_ref, kin_ref, w_ref, bias_ref, key_ref, kall_ref = refs[n_pages:]
    nq = ds * ih
    n_past = n_pages * page
    q2 = jnp.concatenate([qi_ref[:, h * HEAD:(h + 1) * HEAD] for h in range(ih)], axis=0).astype(BF16)
    w = w_ref[...] * scale
    sel = (lax.broadcasted_iota(I32, (nq, LANE), 0) % ds
           == lax.broadcasted_iota(I32, (nq, LANE), 1) % ds).astype(BF16)
    for p in range(n_pages):
        kall_ref[p * page:(p + 1) * page, :] = cik_refs[p][...].astype(BF16)
    kall_ref[n_past:, :] = jnp.concatenate([kin_ref[...], jnp.zeros((page - ds, HEAD), F32)],
                                           axis=0).astype(BF16)
    s = lax.dot_general(kall_ref[...], q2, _NT, preferred_element_type=F32)
    keys = _to_key(_dot2_rhs(jnp.maximum(s, 0.0) * w, sel))
    key_ref[:n_past, :] = keys[:n_past]
    visible = (lax.broadcasted_iota(I32, (page, LANE), 0) <= lax.broadcasted_iota(I32, (page, LANE), 1) % ds)
    key_ref[n_past:, :] = jnp.where(visible, keys[n_past:], KEY_NEG_INF)

    def count_ge(cand):
        total = _count_ge(key_ref[0:page, :], cand)
        for r0 in range(page, n_past + page, page):
            total = total + _count_ge(key_ref[r0:r0 + page, :], cand)
        return total

    thr = _kth_largest_replicated(count_ge, n_sel, ds)
    bias_ref[...] = _select_bias(key_ref[...], thr).astype(bias_ref.dtype)


def _dot2_rhs(x, b):
    hi = x.astype(BF16)
    lo = (x - hi.astype(F32)).astype(BF16)
    return jnp.dot(hi, b, preferred_element_type=F32) + jnp.dot(lo, b, preferred_element_type=F32)


def _index_select_sample(page_table, cik, qi, ki, w_s, dm):
    assert dm.page == LANE and dm.heads * dm.ds == LANE and dm.t % dm.ds == 0
    nq = dm.ds * dm.ih
    n_keys = (dm.n_pages + 1) * dm.page
    assert n_keys >= dm.nsel_s
    r0 = dm.t // dm.ds
    page_spec = lambda p: pl.BlockSpec((None, dm.page, HEAD), lambda b, pt, p=p: (pt[b, p], 0, 0))
    grid_spec = pltpu.PrefetchScalarGridSpec(
        num_scalar_prefetch=1,
        grid=(dm.db,),
        in_specs=[page_spec(p) for p in range(dm.n_pages)] + [
            pl.BlockSpec((dm.ds, dm.ih * HEAD), lambda b, pt: (r0 + b, 0)),
            pl.BlockSpec((dm.ds, HEAD), lambda b, pt: (r0 + b, 0)),
            pl.BlockSpec((None, 1, nq), lambda b, pt: (b, 0, 0))],
        out_specs=pl.BlockSpec((None, n_keys, LANE), lambda b, pt: (b, 0, 0)),
        scratch_shapes=[pltpu.VMEM((n_keys, LANE), I32), pltpu.VMEM((n_keys, HEAD), BF16)],
    )
    return pl.pallas_call(
        functools.partial(_sample_index_kernel, n_pages=dm.n_pages, page=dm.page, ih=dm.ih, ds=dm.ds,
                          n_sel=dm.nsel_s, scale=dm.ih ** -0.5 * HEAD ** -0.5),
        grid_spec=grid_spec,
        out_shape=jax.ShapeDtypeStruct((dm.db, n_keys, LANE), BF16),
        compiler_params=_params("parallel"),
        name="index_select_sample",
    )(page_table, *([cik] * dm.n_pages), qi, ki, w_s)


def _sample_attn_kernel(pt_ref, *refs, pps, page, nh, ds):
    del pt_ref
    k_refs, v_refs = refs[:pps], refs[pps:2 * pps]
    q_ref, kn_ref, vn_ref, bias_ref, o_ref, qt_ref, onehot_ref, headmask_ref, m_ref, l_ref, acc_ref = refs[2 * pps:]
    b, g = pl.program_id(0), pl.program_id(1)
    rows = page * nh

    @pl.when((b == 0) & (g == 0))
    def _():
        r = lax.broadcasted_iota(I32, (rows, LANE), 0)
        c = lax.broadcasted_iota(I32, (rows, LANE), 1)
        onehot_ref[...] = (r // nh == c).astype(BF16)
        headmask_ref[...] = jnp.where(r % nh == c // ds, 0.0, MASKED)

    @pl.when(g == 0)
    def _():
        q8 = q_ref[...]
        qstack = jnp.concatenate([q8[:, h * HEAD:(h + 1) * HEAD] for h in range(nh)], axis=0)
        qt_ref[...] = qstack.T.astype(BF16)
        m_ref[...] = jnp.full_like(m_ref, -jnp.inf)
        l_ref[...] = jnp.zeros_like(l_ref)
        acc_ref[...] = jnp.zeros_like(acc_ref)

    def attend(tiles):
        qt = qt_ref[...]
        scores = []
        for k2, _, onehot, headmask, bias_tile in tiles:
            rhs = jnp.concatenate([qt, bias_tile], axis=0)
            n = k2.shape[0]
            step = min(n, 512)
            scores.append(jnp.concatenate(
                [jnp.dot(jnp.concatenate([k2[r:r + step].astype(BF16), onehot[r:r + step]], axis=1), rhs,
                         preferred_element_type=F32) + headmask[r:r + step] for r in range(0, n, step)], axis=0))
        tile_max = _fold_rows(jnp.concatenate([_fold_rows(s, jnp.maximum) for s in scores], axis=0), jnp.maximum)
        m_old = m_ref[...]
        m_new = jnp.maximum(m_old, jnp.max(tile_max, axis=0, keepdims=True))
        alpha = jnp.exp(m_old - m_new)
        l_add, acc_add = None, None
        for s, (_, v2, _, _, _) in zip(scores, tiles):
            p = jnp.exp(s - m_new)
            l_t = _fold_rows(p, jnp.add)
            acc_t = lax.dot_general(v2.astype(BF16), p.astype(BF16), _TN, preferred_element_type=F32)
            l_add = l_t if l_add is None else l_add + l_t
            acc_add = acc_t if acc_add is None else acc_add + acc_t
        l_ref[...] = alpha * l_ref[...] + jnp.sum(l_add, axis=0, keepdims=True)
        acc_ref[...] = alpha * acc_ref[...] + acc_add
        m_ref[...] = m_new

    attend([(k_refs[p][...], v_refs[p][...], onehot_ref[...], headmask_ref[...],
             bias_ref[pl.ds(pl.multiple_of((g * pps + p) * page, page), page), :]) for p in range(pps)])

    @pl.when(g == pl.num_programs(1) - 1)
    def _():
        n_past = bias_ref.shape[0] - page
        by_head = lambda ref: jnp.concatenate([ref[:, h * HEAD:(h + 1) * HEAD] for h in range(nh)], axis=0)
        r = lax.broadcasted_iota(I32, (nh * ds, LANE), 0)
        c = lax.broadcasted_iota(I32, (nh * ds, LANE), 1)
        attend([(by_head(kn_ref), by_head(vn_ref), (r % ds == c).astype(BF16),
                 jnp.where(r // ds == c // ds, 0.0, MASKED), bias_ref[n_past:, :])])
        out = (acc_ref[...] / l_ref[...]).T
        for h in range(nh):
            o_ref[:, h * HEAD:(h + 1) * HEAD] = out[h * ds:(h + 1) * ds, :]


def _attend_sample(page_table, ck, cv, qa, k_new, v_new, bias, dm):
    assert dm.page == LANE and dm.heads * dm.ds == LANE
    pps = max(p for p in (8, 4, 2, 1) if dm.n_pages % p == 0)
    w = dm.heads * HEAD
    rows = dm.page * dm.heads
    n_keys = (dm.n_pages + 1) * dm.page
    r0 = dm.t // dm.ds
    page_spec = lambda p: pl.BlockSpec((None, rows, HEAD), lambda b, g, pt, p=p: (pt[b, g * pps + p], 0, 0))
    grid_spec = pltpu.PrefetchScalarGridSpec(
        num_scalar_prefetch=1,
        grid=(dm.db, dm.n_pages // pps),
        in_specs=[page_spec(p) for p in range(pps)] * 2 + [
            pl.BlockSpec((dm.ds, w), lambda b, g, pt: (r0 + b, 0)),
            pl.BlockSpec((dm.ds, w), lambda b, g, pt: (b, 0)),
            pl.BlockSpec((dm.ds, w), lambda b, g, pt: (b, 0)),
            pl.BlockSpec((None, n_keys, LANE), lambda b, g, pt: (b, 0, 0))],
        out_specs=pl.BlockSpec((dm.ds, w), lambda b, g, pt: (b, 0)),
        scratch_shapes=[pltpu.VMEM((HEAD, LANE), BF16), pltpu.VMEM((rows, LANE), BF16),
                        pltpu.VMEM((rows, LANE), F32), pltpu.VMEM((1, LANE), F32), pltpu.VMEM((1, LANE), F32),
                        pltpu.VMEM((HEAD, LANE), F32)],
    )
    n_pool = ck.shape[0]
    return pl.pallas_call(
        functools.partial(_sample_attn_kernel, pps=pps, page=dm.page, nh=dm.heads, ds=dm.ds),
        grid_spec=grid_spec,
        out_shape=jax.ShapeDtypeStruct((dm.db * dm.ds, w), F32),
        compiler_params=_params("arbitrary", "arbitrary"),
        name="attend_sample",
    )(page_table, *([ck.reshape(n_pool, rows, HEAD)] * pps), *([cv.reshape(n_pool, rows, HEAD)] * pps),
      qa, k_new, v_new, bias)


def kernel(x_prompt, x_sample, cache_k, cache_v, cache_idx_k, state_hgrn, page_table, meta_tokens, lb_logits,
           norm_mix_g, w_in, q_norm_g, w_uq, w_uq_idx, idx_k_norm_g, idx_k_norm_b, hgrn_norm_g, w_out,
           norm_ffn_g, w_gate, w_up, w_down, final_norm_g):
    assert x_prompt.shape[0] == 1 and w_in.shape[0] == 1, "single prompt sequence, single layer"
    dm = _dims(x_prompt, x_sample, cache_k, page_table)
    d, t, r, rp = dm.d, dm.t, dm.r, dm.rp
    ns = dm.db * dm.ds
    off_cq = 4 * dm.d_r
    off_k = off_cq + dm.q_rank
    off_v = off_k + dm.d_r
    off_ik = off_v + dm.d_r

    h0 = jnp.concatenate([meta_tokens.astype(F32), x_prompt[0], x_sample.reshape(ns, d),
                          jnp.zeros((rp - r, d), F32)], axis=0)
    hn = _rmsnorm(h0, norm_mix_g[0], width=d, col_block=0, out_dtype=BF16)
    w_in_t = w_in[0].T
    z = _matmul_nt(hn, w_in_t, dm, out_dtype=F32, n_cols=off_ik, name="in_proj")
    w_tail = jnp.pad(w_in_t[off_ik:], ((0, 2 * HEAD - (dm.d_in - off_ik)), (0, 0)))
    z_tail = _matmul_nt(hn, w_tail, dm, out_dtype=F32, name="in_proj_idx")

    k_s = z[t:r, off_k:off_k + dm.d_r]
    v_s = z[t:r, off_v:off_v + dm.d_r]

    o_r_p, s_p = _hgrn_prompt(z, lb_logits, hgrn_norm_g[0], dm)
    o_r_s, s_s = _hgrn_sample(z, lb_logits, hgrn_norm_g[0], state_hgrn[0], dm)

    cqn = _rmsnorm(z, q_norm_g[0], width=dm.q_rank, col_block=off_cq // dm.q_rank, out_dtype=BF16)
    qa = _matmul(cqn, w_uq[0], dm, out_dtype=F32, scale=HEAD ** -0.5, name="q_up")
    qi = _matmul(cqn, w_uq_idx[0], dm, out_dtype=F32, name="q_idx_up")
    ki, ki_b, wt = _idx_prep(z_tail, idx_k_norm_g[0], idx_k_norm_b[0], dm, 0)

    ka, vat = _attn_operands(z, off_k, off_v, dm)
    bias_p = _index_select_prompt(qi, ki_b, wt, dm)
    o_a_p = _attend_prompt(qa, ka, vat, bias_p, dm)

    w_s = z_tail[t:r, HEAD:HEAD + dm.ih].reshape(dm.db, dm.ds, dm.ih).swapaxes(1, 2).reshape(dm.db, 1, dm.ds * dm.ih)
    bias_s = _index_select_sample(page_table, cache_idx_k[0], qi, ki, w_s, dm)
    o_a_s = _attend_sample(page_table, cache_k[0], cache_v[0], qa, k_s, v_s, bias_s, dm)

    n_main = t // dm.tm
    r_main = n_main * dm.tm
    tail = jnp.concatenate([
        jnp.concatenate([o_r_p[r_main:t], o_a_p[r_main:t]], axis=1),
        jnp.concatenate([o_r_s, o_a_s], axis=1).astype(BF16),
        jnp.zeros((rp - r, 2 * dm.d_r), BF16)], axis=0)
    if n_main:
        h1 = _out_proj(o_r_p, o_a_p, tail, w_out[0], h0, dm, n_main)
    else:
        h1 = _matmul(tail, w_out[0], dm, out_dtype=F32, residual=h0, name="out_proj")

    hf = _rmsnorm(h1, norm_ffn_g[0], width=d, col_block=0, out_dtype=BF16)
    act = _ffn_up(hf, w_gate[0], w_up[0], dm)
    h2 = _matmul(act, w_down[0].astype(BF16), dm, out_dtype=F32, residual=h1, tn_max=256, name="ffn_down")
    y_prompt = _rmsnorm_rows(h2, final_norm_g, N_META, t - N_META)
    y_sample = _rmsnorm_rows(h2, final_norm_g, t, ns)

    sample_heads = lambda a: a.reshape(1, dm.db, dm.ds, dm.heads, HEAD)
    return (y_prompt[None],
            y_sample.reshape(dm.db, dm.ds, d),
            _prompt_heads(z, off_k, dm)[None, None], _prompt_heads(z, off_v, dm)[None, None],
            ki[:t].reshape(1, 1, t, HEAD), s_p[None, None],
            sample_heads(k_s), sample_heads(v_s), ki[t:r].reshape(1, dm.db, dm.ds, HEAD), s_s[None])
```
